```python
import functools
import jax, jax.numpy as jnp
from jax import lax
import numpy as np

D_MODEL = 1024
BATCH = 4
SEQ = 4096
DEPTH = 2
DEC_BATCH = 128
DEC_SEQ = 1
PAST_LEN = 16384
PAGE_SIZE = 128

HEAD_DIM = 64
BLOCK_Q = 128
EPS = 1e-6
NEG_INF = -1e30

SB_HEADS = 8
SB_KV_HEADS = 2
SB_GROUP = SB_HEADS // SB_KV_HEADS
SB_WIDTH = SB_HEADS * HEAD_DIM
SB_KV_WIDTH = SB_KV_HEADS * HEAD_DIM

GM_GROUPS = 8
GM_GROUP_DIM = 64
GM_CHUNK = 128
GM_WIDTH = GM_GROUPS * GM_GROUP_DIM

MLA_HEADS = 8
MLA_NOPE = 64
MLA_ROPE = 32
MLA_QK = MLA_NOPE + MLA_ROPE
MLA_VDIM = 64
MLA_Q_LORA = 768
MLA_KV_LORA = 256
MLA_WIDTH = MLA_HEADS * MLA_VDIM
ROPE_THETA = 10000.0

N_BRANCH = 3
IN_SIZES = (SB_WIDTH, SB_KV_WIDTH, SB_KV_WIDTH, GM_WIDTH, GM_WIDTH, MLA_Q_LORA, MLA_KV_LORA, MLA_ROPE, N_BRANCH * D_MODEL)
IN_COLS = SB_WIDTH + 2 * SB_KV_WIDTH + 2 * GM_WIDTH + MLA_Q_LORA + MLA_KV_LORA + MLA_ROPE + N_BRANCH * D_MODEL

D_FF = 2816
N_EXPERTS = 8
TOP_K = 2
D_FF_EXPERT = 3584
N_DENSE = (DEPTH + 1) // 2
N_MOE = DEPTH // 2

kernel_name = 'hybrid_sb_gmlp_mla_decoder_step'


def rms_norm(x, g):
    xf = x.astype(jnp.float32)
    y = xf * lax.rsqrt(jnp.mean(xf * xf, axis=-1, keepdims=True) + EPS)
    return (y * g.astype(jnp.float32)).astype(x.dtype)


def split_columns(z):
    offsets, acc = [], 0
    for size in IN_SIZES[:-1]:
        acc += size
        offsets.append(acc)
    return jnp.split(z, offsets, axis=-1)


def rope_tables(pos):
    inv_freq = ROPE_THETA ** (-jnp.arange(0, MLA_ROPE, 2, dtype=jnp.float32) / MLA_ROPE)
    ang = pos[:, None] * inv_freq[None, :]
    return jnp.cos(ang), jnp.sin(ang)


def apply_rope(x, cos, sin):
    half = MLA_ROPE // 2
    bshape = (cos.shape[0],) + (1,) * (x.ndim - 3) + (half,)
    c, s = cos.reshape(bshape), sin.reshape(bshape)
    xf = x.astype(jnp.float32)
    x1, x2 = xf[..., :half], xf[..., half:]
    return jnp.concatenate([x1 * c - x2 * s, x1 * s + x2 * c], axis=-1).astype(x.dtype)


def mla_head_norm(nope, rot, g):
    rot = jnp.broadcast_to(rot, nope.shape[:-1] + (MLA_ROPE,))
    h = jnp.concatenate([nope, rot.astype(nope.dtype)], axis=-1).astype(jnp.float32)
    h = h * lax.rsqrt(jnp.mean(h * h, axis=-1, keepdims=True) + EPS)
    gain = jnp.concatenate([g.astype(jnp.float32), jnp.ones((MLA_ROPE,), jnp.float32)])
    return (h * gain).astype(nope.dtype)


def sb_log_terms(z, mask):
    log_beta = jax.nn.log_sigmoid(z)
    log_keep = jnp.where(mask, jax.nn.log_sigmoid(-z), 0.0)
    after = lax.cumsum(log_keep, axis=z.ndim - 1, reverse=True) - log_keep
    return log_beta, log_keep, after


def sb_attend_prompt(q, k, v):
    T = q.shape[1]
    scale = HEAD_DIM ** -0.5
    outs = []
    for start in range(0, T, BLOCK_Q):
        end = min(start + BLOCK_Q, T)
        z = jnp.einsum('bqhgd,bkhd->bhgqk', q[:, start:end], k[:, :end],
                       preferred_element_type=jnp.float32) * scale
        mask = jnp.arange(end)[None, :] < jnp.arange(start, end)[:, None]
        log_beta, _, after = sb_log_terms(z, mask)
        w = jnp.where(mask, jnp.exp(log_beta + after), 0.0)
        outs.append(jnp.einsum('bhgqk,bkhd->bqhgd', w, v[:, :end], preferred_element_type=jnp.float32))
    return jnp.concatenate(outs, axis=1).astype(q.dtype)


def sb_attend_sample(q, k, v, cache_k, cache_v, page_table, layer):
    T = q.shape[1]
    scale = HEAD_DIM ** -0.5
    z = jnp.einsum('bqhgd,bkhd->bhgqk', q, k, preferred_element_type=jnp.float32) * scale
    mask = jnp.arange(T)[None, :] < jnp.arange(T)[:, None]
    log_beta, log_keep, after = sb_log_terms(z, mask)
    w = jnp.where(mask, jnp.exp(log_beta + after), 0.0)
    out = jnp.einsum('bhgqk,bkhd->bqhgd', w, v, preferred_element_type=jnp.float32)
    keep = jnp.sum(log_keep, axis=-1)

    def page_step(carry, pages):
        out, keep = carry
        kp = cache_k[layer, pages]
        vp = cache_v[layer, pages]
        z = jnp.einsum('bqhgd,bkhd->bhgqk', q, kp, preferred_element_type=jnp.float32) * scale
        log_beta, log_keep, after = sb_log_terms(z, True)
        w = jnp.exp(log_beta + after + keep[..., None])
        out = out + jnp.einsum('bhgqk,bkhd->bqhgd', w, vp, preferred_element_type=jnp.float32)
        return (out, keep + jnp.sum(log_keep, axis=-1)), None

    (out, _), _ = lax.scan(page_step, (out, keep), page_table.T, reverse=True)
    return out.astype(q.dtype)


def chunk_gmlp(u, v, w_s, b_s):
    B_, T, _ = v.shape
    n_chunks = -(-T // GM_CHUNK)
    pad = n_chunks * GM_CHUNK - T
    vc = jnp.pad(v, ((0, 0), (0, pad), (0, 0))).reshape(B_, n_chunks, GM_CHUNK, GM_GROUPS, GM_GROUP_DIM)
    causal = jnp.arange(GM_CHUNK)[None, :] <= jnp.arange(GM_CHUNK)[:, None]
    w = jnp.where(causal, w_s, 0.0)
    mixed = jnp.einsum('gts,bcsgd->bctgd', w, vc) + jnp.transpose(b_s)[:, :, None]
    mixed = mixed.reshape(B_, n_chunks * GM_CHUNK, GM_WIDTH)[:, :T]
    return u * mixed


def mla_attend_prompt(q, k, v):
    T = q.shape[1]
    scale = MLA_QK ** -0.5
    outs = []
    for start in range(0, T, BLOCK_Q):
        end = min(start + BLOCK_Q, T)
        s = jnp.einsum('bqhd,bkhd->bhqk', q[:, start:end], k[:, :end],
                       preferred_element_type=jnp.float32) * scale
        mask = jnp.arange(end)[None, :] <= jnp.arange(start, end)[:, None]
        p = jax.nn.softmax(jnp.where(mask, s, NEG_INF), axis=-1)
        outs.append(jnp.einsum('bhqk,bkhd->bqhd', p, v[:, :end], preferred_element_type=jnp.float32))
    return jnp.concatenate(outs, axis=1).astype(q.dtype)


def mla_attend_sample(q, k, v, cache_ckv, cache_krope, page_table, w_uk, w_uv, g_k, layer):
    T = q.shape[1]
    scale = MLA_QK ** -0.5
    s = jnp.einsum('bqhd,bkhd->bhqk', q, k, preferred_element_type=jnp.float32) * scale
    mask = jnp.arange(T)[None, :] <= jnp.arange(T)[:, None]
    s = jnp.where(mask, s, NEG_INF)
    m = jnp.max(s, axis=-1)
    p = jnp.exp(s - m[..., None])
    l = jnp.sum(p, axis=-1)
    acc = jnp.einsum('bhqk,bkhd->bhqd', p, v, preferred_element_type=jnp.float32)

    def page_step(carry, pages):
        m, l, acc = carry
        ckv = cache_ckv[layer, pages]
        krot = cache_krope[layer, pages]
        kp = mla_head_norm(jnp.einsum('bsc,chd->bshd', ckv, w_uk), krot[:, :, None, :], g_k)
        vp = jnp.einsum('bsc,chd->bshd', ckv, w_uv)
        s = jnp.einsum('bqhd,bkhd->bhqk', q, kp, preferred_element_type=jnp.float32) * scale
        m_new = jnp.maximum(m, jnp.max(s, axis=-1))
        alpha = jnp.exp(m - m_new)
        p = jnp.exp(s - m_new[..., None])
        l = l * alpha + jnp.sum(p, axis=-1)
        acc = acc * alpha[..., None] + jnp.einsum('bhqk,bkhd->bhqd', p, vp, preferred_element_type=jnp.float32)
        return (m_new, l, acc), None

    (m, l, acc), _ = lax.scan(page_step, (m, l, acc), page_table.T)
    out = acc / l[..., None]
    return jnp.transpose(out, (0, 2, 1, 3)).astype(q.dtype)


def token_mixer(x, pos, sb_fn, mla_fn, g_mix, w_in, g_sb_q, g_sb_k, g_gm_v, w_gm_s, b_gm_s,
                g_mla_cq, g_mla_ckv, w_mla_uq, w_mla_uk, w_mla_uv, g_mla_q, g_mla_k,
                w_br_a, w_br_b, w_br_c, w_out):
    B_, T, _ = x.shape
    h = rms_norm(x, g_mix)
    sb_q, sb_k, sb_v, gm_u, gm_v, c_q, c_kv, k_rope, gate_logits = split_columns(h @ w_in)
    q_a = rms_norm(sb_q.reshape(B_, T, SB_KV_HEADS, SB_GROUP, HEAD_DIM), g_sb_q)
    k_a = rms_norm(sb_k.reshape(B_, T, SB_KV_HEADS, HEAD_DIM), g_sb_k)
    v_a = sb_v.reshape(B_, T, SB_KV_HEADS, HEAD_DIM)
    o_a = sb_fn(q_a, k_a, v_a).reshape(B_, T, SB_WIDTH)
    u_b = jax.nn.gelu(gm_u)
    v_b = rms_norm(jax.nn.gelu(gm_v), g_gm_v)
    o_b = chunk_gmlp(u_b, v_b, w_gm_s, b_gm_s)
    cos, sin = rope_tables(pos)
    q_c = jnp.einsum('btc,chd->bthd', rms_norm(c_q, g_mla_cq), w_mla_uq)
    q_c = mla_head_norm(q_c[..., :MLA_NOPE], apply_rope(q_c[..., MLA_NOPE:], cos, sin), g_mla_q)
    ckv = rms_norm(c_kv, g_mla_ckv)
    k_rot = apply_rope(k_rope, cos, sin)
    k_c = mla_head_norm(jnp.einsum('btc,chd->bthd', ckv, w_mla_uk), k_rot[:, :, None, :], g_mla_k)
    v_c = jnp.einsum('btc,chd->bthd', ckv, w_mla_uv)
    o_c = mla_fn(q_c, k_c, v_c).reshape(B_, T, MLA_WIDTH)
    gates = jax.nn.sigmoid(gate_logits.astype(jnp.float32)).astype(x.dtype).reshape(B_, T, N_BRANCH, D_MODEL)
    merged = (gates[:, :, 0] * (o_a @ w_br_a) + gates[:, :, 1] * (o_b @ w_br_b)
              + gates[:, :, 2] * (o_c @ w_br_c))
    return x + merged @ w_out, (k_a, v_a, ckv, k_rot, v_b)


def swiglu(h, w_gate, w_up, w_down):
    return (jax.nn.silu(h @ w_gate) * (h @ w_up)) @ w_down


def moe_swiglu(h, w_router, w_gate, w_up, w_down):
    logits = jnp.einsum('btd,de->bte', h, w_router, preferred_element_type=jnp.float32)
    top_logit, top_idx = lax.top_k(logits, TOP_K)
    top_w = jax.nn.softmax(top_logit, axis=-1)
    combine = jnp.sum(jax.nn.one_hot(top_idx, N_EXPERTS, dtype=jnp.float32) * top_w[..., None], axis=-2)
    out = jnp.zeros_like(h)
    for e in range(N_EXPERTS):
        out = out + combine[..., e:e + 1].astype(h.dtype) * swiglu(h, w_gate[e], w_up[e], w_down[e])
    return out


def channel_mixer(x, layer, g_ffn, w_ffn_gate, w_ffn_up, w_ffn_down, w_router, w_exp_gate, w_exp_up, w_exp_down):
    h = rms_norm(x, g_ffn[layer])
    i = layer // 2
    if layer % 2 == 0:
        return x + swiglu(h, w_ffn_gate[i], w_ffn_up[i], w_ffn_down[i])
    return x + moe_swiglu(h, w_router[i], w_exp_gate[i], w_exp_up[i], w_exp_down[i])


def setup_inputs(seed: int = 0) -> dict:
    key = jax.random.key(seed)
    keys = iter(jax.random.split(key, 48))
    n_pages = PAST_LEN // PAGE_SIZE
    n_pool = (DEC_BATCH * n_pages * 5) // 4

    def normal(shape, scale):
        return jax.random.normal(next(keys), shape, jnp.float32) * scale

    def gain(shape):
        return 1.0 + 0.05 * jax.random.normal(next(keys), shape, jnp.float32)

    page_table = jax.random.permutation(next(keys), n_pool)[: DEC_BATCH * n_pages]
    page_table = page_table.reshape(DEC_BATCH, n_pages).astype(jnp.int32)
    return {
        'x_prompt': normal((BATCH, SEQ, D_MODEL), 1.0),
        'x_sample': normal((DEC_BATCH, DEC_SEQ, D_MODEL), 1.0),
        'cache_sb_k': normal((DEPTH, n_pool, PAGE_SIZE, SB_KV_HEADS, HEAD_DIM), 1.0),
        'cache_sb_v': normal((DEPTH, n_pool, PAGE_SIZE, SB_KV_HEADS, HEAD_DIM), 1.0),
        'cache_mla_ckv': normal((DEPTH, n_pool, PAGE_SIZE, MLA_KV_LORA), 1.0),
        'cache_mla_krope': normal((DEPTH, n_pool, PAGE_SIZE, MLA_ROPE), 1.0),
        'page_table': page_table,
        'g_mix': gain((DEPTH, D_MODEL)),
        'w_in': normal((DEPTH, D_MODEL, IN_COLS), D_MODEL ** -0.5),
        'g_sb_q': gain((DEPTH, HEAD_DIM)),
        'g_sb_k': gain((DEPTH, HEAD_DIM)),
        'g_gm_v': gain((DEPTH, GM_WIDTH)),
        'w_gm_s': normal((DEPTH, GM_GROUPS, GM_CHUNK, GM_CHUNK), GM_CHUNK ** -0.5),
        'b_gm_s': 1.0 + normal((DEPTH, GM_GROUPS, GM_CHUNK), 0.1),
        'g_mla_cq': gain((DEPTH, MLA_Q_LORA)),
        'g_mla_ckv': gain((DEPTH, MLA_KV_LORA)),
        'w_mla_uq': normal((DEPTH, MLA_Q_LORA, MLA_HEADS, MLA_QK), MLA_Q_LORA ** -0.5),
        'w_mla_uk': normal((DEPTH, MLA_KV_LORA, MLA_HEADS, MLA_NOPE), MLA_KV_LORA ** -0.5),
        'w_mla_uv': normal((DEPTH, MLA_KV_LORA, MLA_HEADS, MLA_VDIM), MLA_KV_LORA ** -0.5),
        'g_mla_q': gain((DEPTH, MLA_NOPE)),
        'g_mla_k': gain((DEPTH, MLA_NOPE)),
        'w_br_a': normal((DEPTH, SB_WIDTH, D_MODEL), SB_WIDTH ** -0.5),
        'w_br_b': normal((DEPTH, GM_WIDTH, D_MODEL), GM_WIDTH ** -0.5),
        'w_br_c': normal((DEPTH, MLA_WIDTH, D_MODEL), MLA_WIDTH ** -0.5),
        'w_out': normal((DEPTH, D_MODEL, D_MODEL), D_MODEL ** -0.5),
        'g_ffn': gain((DEPTH, D_MODEL)),
        'w_ffn_gate': normal((N_DENSE, D_MODEL, D_FF), D_MODEL ** -0.5),
        'w_ffn_up': normal((N_DENSE, D_MODEL, D_FF), D_MODEL ** -0.5),
        'w_ffn_down': normal((N_DENSE, D_FF, D_MODEL), D_FF ** -0.5),
        'w_router': normal((N_MOE, D_MODEL, N_EXPERTS), D_MODEL ** -0.5),
        'w_exp_gate': normal((N_MOE, N_EXPERTS, D_MODEL, D_FF_EXPERT), D_MODEL ** -0.5),
        'w_exp_up': normal((N_MOE, N_EXPERTS, D_MODEL, D_FF_EXPERT), D_MODEL ** -0.5),
        'w_exp_down': normal((N_MOE, N_EXPERTS, D_FF_EXPERT, D_MODEL), D_FF_EXPERT ** -0.5),
    }


def reference(x_prompt, x_sample, cache_sb_k, cache_sb_v, cache_mla_ckv, cache_mla_krope, page_table,
              g_mix, w_in, g_sb_q, g_sb_k, g_gm_v, w_gm_s, b_gm_s, g_mla_cq, g_mla_ckv,
              w_mla_uq, w_mla_uk, w_mla_uv, g_mla_q, g_mla_k, w_br_a, w_br_b, w_br_c, w_out,
              g_ffn, w_ffn_gate, w_ffn_up, w_ffn_down, w_router, w_exp_gate, w_exp_up, w_exp_down):
    pos_p = jnp.arange(x_prompt.shape[1], dtype=jnp.float32)
    pos_s = jnp.arange(x_sample.shape[1], dtype=jnp.float32) + PAST_LEN
    h_p, h_s = x_prompt, x_sample
    sbk_p, sbv_p, ckv_p, kr_p = [], [], [], []
    sbk_s, sbv_s, ckv_s, kr_s, gmv_s = [], [], [], [], []
    for l in range(DEPTH):
        mix_w = (g_mix[l], w_in[l], g_sb_q[l], g_sb_k[l], g_gm_v[l], w_gm_s[l], b_gm_s[l],
                 g_mla_cq[l], g_mla_ckv[l], w_mla_uq[l], w_mla_uk[l], w_mla_uv[l], g_mla_q[l], g_mla_k[l],
                 w_br_a[l], w_br_b[l], w_br_c[l], w_out[l])
        sb_s = functools.partial(sb_attend_sample, cache_k=cache_sb_k, cache_v=cache_sb_v,
                                 page_table=page_table, layer=l)
        mla_s = functools.partial(mla_attend_sample, cache_ckv=cache_mla_ckv, cache_krope=cache_mla_krope,
                                  page_table=page_table, w_uk=w_mla_uk[l], w_uv=w_mla_uv[l],
                                  g_k=g_mla_k[l], layer=l)
        h_p, (ka, va, ckv, kr, _) = token_mixer(h_p, pos_p, sb_attend_prompt, mla_attend_prompt, *mix_w)
        sbk_p.append(ka); sbv_p.append(va); ckv_p.append(ckv); kr_p.append(kr)
        h_s, (ka, va, ckv, kr, vb) = token_mixer(h_s, pos_s, sb_s, mla_s, *mix_w)
        sbk_s.append(ka); sbv_s.append(va); ckv_s.append(ckv); kr_s.append(kr); gmv_s.append(vb)
        h_p = channel_mixer(h_p, l, g_ffn, w_ffn_gate, w_ffn_up, w_ffn_down, w_router, w_exp_gate, w_exp_up, w_exp_down)
        h_s = channel_mixer(h_s, l, g_ffn, w_ffn_gate, w_ffn_up, w_ffn_down, w_router, w_exp_gate, w_exp_up, w_exp_down)
    return (h_p, h_s, jnp.stack(sbk_p), jnp.stack(sbv_p), jnp.stack(ckv_p), jnp.stack(kr_p),
            jnp.stack(sbk_s), jnp.stack(sbv_s), jnp.stack(ckv_s), jnp.stack(kr_s), jnp.stack(gmv_s))
```

```python
import functools

import numpy as np
import jax
import jax.numpy as jnp
from jax import lax
from jax.experimental import pallas as pl
from jax.experimental.pallas import tpu as pltpu

F32 = jnp.float32
BF16 = jnp.bfloat16
EPS = 1e-6
NEG_INF = -1e30
LANES = 128

D_MODEL = 1024
HEAD_DIM = 64
SB_HEADS = 8
SB_KV_HEADS = 2
SB_GROUP = SB_HEADS // SB_KV_HEADS
SB_WIDTH = SB_HEADS * HEAD_DIM
SB_KV_WIDTH = SB_KV_HEADS * HEAD_DIM
GM_GROUPS = 8
GM_GROUP_DIM = 64
GM_CHUNK = 128
GM_WIDTH = GM_GROUPS * GM_GROUP_DIM
MLA_HEADS = 8
MLA_NOPE = 64
MLA_ROPE = 32
MLA_QK = MLA_NOPE + MLA_ROPE
MLA_VDIM = 64
MLA_Q_LORA = 768
MLA_KV_LORA = 256
MLA_WIDTH = MLA_HEADS * MLA_VDIM
MLA_HEAD_PAD = LANES
ROPE_THETA = 10000.0
N_BRANCH = 3
N_EXPERTS = 8
TOP_K = 2
PAGE_SIZE = 128

Z_SB_Q = 0
Z_SB_K = Z_SB_Q + SB_WIDTH
Z_SB_V = Z_SB_K + SB_KV_WIDTH
Z_GM_U = Z_SB_V + SB_KV_WIDTH
Z_GM_V = Z_GM_U + GM_WIDTH
Z_CQ = Z_GM_V + GM_WIDTH
Z_CKV = Z_CQ + MLA_Q_LORA
Z_KROPE = Z_CKV + MLA_KV_LORA
Z_MIX = 3 * D_MODEL
Z_COLS = Z_MIX + N_BRANCH * D_MODEL

VMEM_LIMIT = 56 * 1024 * 1024


def _params(*sem, vmem=VMEM_LIMIT):
    return pltpu.CompilerParams(dimension_semantics=sem, vmem_limit_bytes=vmem)


def _dot(a, b):
    return jnp.dot(a, b, preferred_element_type=F32)


def _dot_nt(a, b):
    return lax.dot_general(a, b, (((1,), (1,)), ((), ())), preferred_element_type=F32)


def _split_bf16(x):
    hi = x.astype(BF16)
    lo = (x - hi.astype(F32)).astype(BF16)
    return hi, lo


def _norm_matmul_kernel(x_ref, g_ref, w_ref, o_ref, h_ref):
    @pl.when(pl.program_id(1) == 0)
    def _():
        x = x_ref[...]
        ms = jnp.mean(x * x, axis=-1, keepdims=True)
        h_ref[...] = (x * lax.rsqrt(ms + EPS) * g_ref[...]).astype(BF16)

    o_ref[...] = _dot(h_ref[...], w_ref[...])


def norm_matmul(x, g, w, *, tm, tn):
    m, k = x.shape
    n = w.shape[1]
    return pl.pallas_call(
        _norm_matmul_kernel,
        grid=(m // tm, n // tn),
        in_specs=[pl.BlockSpec((tm, k), lambda i, j: (i, 0)),
                  pl.BlockSpec((1, k), lambda i, j: (0, 0)),
                  pl.BlockSpec((k, tn), lambda i, j: (0, j))],
        out_specs=pl.BlockSpec((tm, tn), lambda i, j: (i, j)),
        out_shape=jax.ShapeDtypeStruct((m, n), F32),
        scratch_shapes=[pltpu.VMEM((tm, k), BF16)],
        compiler_params=_params("parallel", "arbitrary"),
        name="norm_matmul",
    )(x, g.reshape(1, k), w)


def _gelu(x):
    c = np.float32(np.sqrt(2.0 / np.pi))
    return x * (0.5 * (1.0 + jnp.tanh(c * (x + 0.044715 * (x * x * x)))))


def _rope(x, c, s1, s2, width):
    return x * c + pltpu.roll(x, width - MLA_ROPE // 2, 1) * s1 + pltpu.roll(x, MLA_ROPE // 2, 1) * s2


def _group_sumsq(x, bd):
    hi, lo = _split_bf16(x * x)
    return _dot(hi, bd) + _dot(lo, bd)


def _prep_kernel(z_ref, gq_ref, gk_ref, gv_ref, gcq_ref, gckv_ref, c_ref, s1_ref, s2_ref, bdq_ref, bdk_ref,
                 qa_ref, ka_ref, va_ref, ub_ref, vb_ref, cqn_ref, ckv_ref, krot_ref):
    inv_d = 1.0 / HEAD_DIM
    q = z_ref[:, Z_SB_Q:Z_SB_Q + SB_WIDTH]
    qa_ref[...] = (q * lax.rsqrt(_group_sumsq(q, bdq_ref[...]) * inv_d + EPS) * gq_ref[...]).astype(qa_ref.dtype)
    k = z_ref[:, Z_SB_K:Z_SB_K + SB_KV_WIDTH]
    ka_ref[...] = k * lax.rsqrt(_group_sumsq(k, bdk_ref[...]) * inv_d + EPS) * gk_ref[...]
    va_ref[...] = z_ref[:, Z_SB_V:Z_SB_V + SB_KV_WIDTH]
    ub_ref[...] = _gelu(z_ref[:, Z_GM_U:Z_GM_U + GM_WIDTH]).astype(ub_ref.dtype)
    gv = _gelu(z_ref[:, Z_GM_V:Z_GM_V + GM_WIDTH])
    vb_ref[...] = gv * lax.rsqrt(jnp.mean(gv * gv, axis=-1, keepdims=True) + EPS) * gv_ref[...]
    cq = z_ref[:, Z_CQ:Z_CQ + MLA_Q_LORA]
    cqn_ref[...] = (cq * lax.rsqrt(jnp.mean(cq * cq, axis=-1, keepdims=True) + EPS) * gcq_ref[...]).astype(cqn_ref.dtype)
    ckv = z_ref[:, Z_CKV:Z_CKV + MLA_KV_LORA]
    ckv_ref[...] = ckv * lax.rsqrt(jnp.mean(ckv * ckv, axis=-1, keepdims=True) + EPS) * gckv_ref[...]
    kr = z_ref[:, Z_KROPE:Z_KROPE + LANES]
    krot_ref[...] = _rope(kr, c_ref[...], s1_ref[...], s2_ref[...], LANES)


def _block_diag_ones(width, group):
    idx = np.arange(width) // group
    return jnp.asarray(idx[:, None] == idx[None, :], BF16)


def prep(z, gq, gk, gv, gcq, gckv, tabs, *, tm):
    m = z.shape[0]
    nt = tabs[0].shape[0] // tm
    row = lambda w: pl.BlockSpec((tm, w), lambda i: (i, 0))
    vec = lambda w: pl.BlockSpec((1, w), lambda i: (0, 0))
    tab = pl.BlockSpec((tm, LANES), lambda i: (i % nt, 0))
    full = lambda w: pl.BlockSpec((w, w), lambda i: (0, 0))
    outs = [(SB_WIDTH, BF16), (SB_KV_WIDTH, F32), (SB_KV_WIDTH, F32), (GM_WIDTH, BF16), (GM_WIDTH, F32),
            (MLA_Q_LORA, BF16), (MLA_KV_LORA, F32), (LANES, F32)]
    return pl.pallas_call(
        _prep_kernel,
        grid=(m // tm,),
        in_specs=[pl.BlockSpec((tm, Z_MIX), lambda i: (i, 0)),
                  vec(SB_WIDTH), vec(SB_KV_WIDTH), vec(GM_WIDTH), vec(MLA_Q_LORA), vec(MLA_KV_LORA),
                  tab, tab, tab, full(SB_WIDTH), full(SB_KV_WIDTH)],
        out_specs=[row(w) for w, _ in outs],
        out_shape=[jax.ShapeDtypeStruct((m, w), dt) for w, dt in outs],
        compiler_params=_params("parallel"),
        name="prep",
    )(z, gq, gk, gv, gcq, gckv, *tabs, _block_diag_ones(SB_WIDTH, HEAD_DIM), _block_diag_ones(SB_KV_WIDTH, HEAD_DIM))


def _head_norm(xh, gain):
    ss = jnp.sum(xh * xh, axis=-1, keepdims=True) * (1.0 / MLA_QK)
    return xh * lax.rsqrt(ss + EPS) * gain


def _mla_q_kernel(cq_ref, w_ref, gain_ref, c_ref, s1_ref, s2_ref, o_ref):
    q = _dot(cq_ref[...], w_ref[...])
    c, s1, s2 = c_ref[...], s1_ref[...], s2_ref[...]
    for h in range(MLA_HEADS):
        sl = slice(h * MLA_HEAD_PAD, (h + 1) * MLA_HEAD_PAD)
        xh = _rope(q[:, sl], c, s1, s2, LANES)
        o_ref[:, sl] = _head_norm(xh, gain_ref[:, sl]).astype(o_ref.dtype)


def mla_q(cqn, w_uq_pad, gain, tabs, *, tm, out_dtype):
    m = cqn.shape[0]
    nt = tabs[0].shape[0] // tm
    width = MLA_HEADS * MLA_HEAD_PAD
    tab = pl.BlockSpec((tm, LANES), lambda i: (i % nt, 0))
    return pl.pallas_call(
        _mla_q_kernel,
        grid=(m // tm,),
        in_specs=[pl.BlockSpec((tm, MLA_Q_LORA), lambda i: (i, 0)),
                  pl.BlockSpec((MLA_Q_LORA, width), lambda i: (0, 0)),
                  pl.BlockSpec((1, width), lambda i: (0, 0)), tab, tab, tab],
        out_specs=pl.BlockSpec((tm, width), lambda i: (i, 0)),
        out_shape=jax.ShapeDtypeStruct((m, width), out_dtype),
        compiler_params=_params("parallel"),
        name="mla_q",
    )(cqn, w_uq_pad, gain, *tabs)


def _mla_kv_kernel(ckv_ref, krot_ref, wk_ref, wv_ref, gain_ref, k_ref, v_ref):
    ckv = ckv_ref[...].astype(BF16)
    kn = _dot(ckv, wk_ref[...])
    krot = krot_ref[...]
    for h in range(MLA_HEADS):
        sl = slice(h * MLA_HEAD_PAD, (h + 1) * MLA_HEAD_PAD)
        k_ref[:, sl] = _head_norm(kn[:, sl] + krot, gain_ref[:, sl]).astype(k_ref.dtype)
    v_ref[...] = _dot(ckv, wv_ref[...]).astype(v_ref.dtype)


def mla_kv(ckv, krot, w_uk_pad, w_uv, gain, *, tm, out_dtype):
    m = ckv.shape[0]
    width = MLA_HEADS * MLA_HEAD_PAD
    return pl.pallas_call(
        _mla_kv_kernel,
        grid=(m // tm,),
        in_specs=[pl.BlockSpec((tm, MLA_KV_LORA), lambda i: (i, 0)),
                  pl.BlockSpec((tm, LANES), lambda i: (i, 0)),
                  pl.BlockSpec((MLA_KV_LORA, width), lambda i: (0, 0)),
                  pl.BlockSpec((MLA_KV_LORA, MLA_WIDTH), lambda i: (0, 0)),
                  pl.BlockSpec((1, width), lambda i: (0, 0))],
        out_specs=[pl.BlockSpec((tm, width), lambda i: (i, 0)), pl.BlockSpec((tm, MLA_WIDTH), lambda i: (i, 0))],
        out_shape=[jax.ShapeDtypeStruct((m, width), out_dtype), jax.ShapeDtypeStruct((m, MLA_WIDTH), out_dtype)],
        compiler_params=_params("parallel"),
        name="mla_kv",
    )(ckv, krot, w_uk_pad, w_uv, gain)


def _sb_block(z, mask, keep, u):
    l1p = jnp.log1p(jnp.exp(-jnp.abs(z)))
    log_beta = jnp.minimum(z, 0.0) - l1p
    log_keep = jnp.minimum(-z, 0.0) - l1p
    if mask is not None:
        log_keep = jnp.where(mask, log_keep, 0.0)
    hi, lo = _split_bf16(log_keep)
    after = _dot(hi, u) + _dot(lo, u)
    w = jnp.exp(log_beta + after + keep)
    if mask is not None:
        w = jnp.where(mask, w, 0.0)
    return w, keep + (after[:, 0:1] + log_keep[:, 0:1])


def _later_keys_ones(n):
    return jnp.asarray(np.arange(n)[:, None] > np.arange(n)[None, :], BF16)


def _sb_prompt_kernel(q_ref, k_ref, v_ref, u_ref, o_ref, *, bq):
    i = pl.program_id(1)
    u = u_ref[...]
    rows = SB_GROUP * bq
    r = lax.broadcasted_iota(jnp.int32, (rows, bq), 0) & (bq - 1)
    s = lax.broadcasted_iota(jnp.int32, (rows, bq), 1)
    diag_mask = s < r
    for h in range(SB_KV_HEADS):
        lanes = slice(h * HEAD_DIM, (h + 1) * HEAD_DIM)
        qh = jnp.concatenate(
            [q_ref[:, (h * SB_GROUP + g) * HEAD_DIM:(h * SB_GROUP + g + 1) * HEAD_DIM] for g in range(SB_GROUP)], axis=0)

        def block(j, mask, keep):
            kb = k_ref[pl.ds(j * bq, bq), lanes].astype(BF16)
            vb = v_ref[pl.ds(j * bq, bq), lanes].astype(BF16)
            w, keep = _sb_block(_dot_nt(qh, kb), mask, keep, u)
            return _dot(w.astype(BF16), vb), keep

        acc, keep = block(i, diag_mask, jnp.zeros((rows, 1), F32))

        def body(jj, carry):
            acc, keep = carry
            pv, keep = block(i - jj, None, keep)
            return acc + pv, keep

        acc, _ = lax.fori_loop(1, i + 1, body, (acc, keep))
        for g in range(SB_GROUP):
            col = (h * SB_GROUP + g) * HEAD_DIM
            o_ref[:, col:col + HEAD_DIM] = acc[g * bq:(g + 1) * bq].astype(o_ref.dtype)


def sb_prompt(q, k, v, *, batch, bq=128):
    m = q.shape[0]
    t = m // batch
    nq = t // bq
    return pl.pallas_call(
        functools.partial(_sb_prompt_kernel, bq=bq),
        grid=(batch, nq),
        in_specs=[pl.BlockSpec((bq, SB_WIDTH), lambda b, i: (b * nq + i, 0)),
                  pl.BlockSpec((t, SB_KV_WIDTH), lambda b, i: (b, 0)),
                  pl.BlockSpec((t, SB_KV_WIDTH), lambda b, i: (b, 0)),
                  pl.BlockSpec((bq, bq), lambda b, i: (0, 0))],
        out_specs=pl.BlockSpec((bq, SB_WIDTH), lambda b, i: (b * nq + i, 0)),
        out_shape=jax.ShapeDtypeStruct((m, SB_WIDTH), BF16),
        compiler_params=_params("parallel", "arbitrary"),
        name="sb_prompt",
    )(q, k, v, _later_keys_ones(bq))


def _sb_decode_kernel(pt_ref, q_ref, u_ref, *refs, pages):
    k_refs, v_refs = refs[:pages], refs[pages:2 * pages]
    o_ref, keep_ref = refs[2 * pages], refs[2 * pages + 1]
    c = pl.program_id(1)

    @pl.when(c == 0)
    def _():
        o_ref[...] = jnp.zeros_like(o_ref)
        keep_ref[...] = jnp.zeros_like(keep_ref)

    q = q_ref[...]
    u = u_ref[...]
    acc = o_ref[...]
    keep = keep_ref[:, 0:1]
    for p in range(pages):
        w, keep = _sb_block(_dot(q, k_refs[p][...].astype(BF16)), None, keep, u)
        acc = acc + _dot_nt(w.astype(BF16), v_refs[p][...].astype(BF16))
    o_ref[...] = acc
    keep_ref[...] = jnp.broadcast_to(keep, keep_ref.shape)


def sb_decode(q_bd, cache_k, cache_v, page_table, layer, *, pages=8):
    b, n_pages = page_table.shape
    steps = n_pages // pages
    feature_major = lambda c: jnp.transpose(c, (0, 1, 3, 4, 2)).reshape(c.shape[0], c.shape[1], SB_KV_WIDTH, PAGE_SIZE)
    ck, cv = feature_major(cache_k), feature_major(cache_v)

    def page_spec(p):
        return pl.BlockSpec((None, None, SB_KV_WIDTH, PAGE_SIZE),
                            lambda s, c, pt: (layer, pt[s * n_pages + n_pages - 1 - (c * pages + p)], 0, 0))

    grid_spec = pltpu.PrefetchScalarGridSpec(
        num_scalar_prefetch=1,
        grid=(b, steps),
        in_specs=[pl.BlockSpec((None, SB_HEADS, SB_KV_WIDTH), lambda s, c, pt: (s, 0, 0)),
                  pl.BlockSpec((PAGE_SIZE, PAGE_SIZE), lambda s, c, pt: (0, 0))]
                 + [page_spec(p) for p in range(pages)] * 2,
        out_specs=pl.BlockSpec((None, SB_HEADS, SB_KV_WIDTH), lambda s, c, pt: (s, 0, 0)),
        scratch_shapes=[pltpu.VMEM((SB_HEADS, LANES), F32)],
    )
    return pl.pallas_call(
        functools.partial(_sb_decode_kernel, pages=pages),
        grid_spec=grid_spec,
        out_shape=jax.ShapeDtypeStruct((b, SB_HEADS, SB_KV_WIDTH), F32),
        compiler_params=_params("parallel", "arbitrary"),
        name="sb_decode",
    )(page_table.reshape(-1), q_bd, _later_keys_ones(PAGE_SIZE), *([ck] * pages), *([cv] * pages))


def _gmlp_kernel(u_ref, v_ref, w_ref, b_ref, o_ref, *, chunks):
    t = lax.broadcasted_iota(jnp.int32, (GM_CHUNK, GM_CHUNK), 0)
    s = lax.broadcasted_iota(jnp.int32, (GM_CHUNK, GM_CHUNK), 1)
    causal = s <= t
    group = lax.broadcasted_iota(jnp.int32, (GM_CHUNK, GM_WIDTH), 1) // GM_GROUP_DIM
    ws = [jnp.where(causal, w_ref[g], 0.0).astype(BF16) for g in range(GM_GROUPS)]
    for c in range(chunks):
        rows = slice(c * GM_CHUNK, (c + 1) * GM_CHUNK)
        vc = v_ref[rows, :].astype(BF16)
        mixed = b_ref[...]
        for g in range(GM_GROUPS):
            mixed = mixed + jnp.where(group == g, _dot(ws[g], vc), 0.0)
        o_ref[rows, :] = (u_ref[rows, :].astype(F32) * mixed).astype(o_ref.dtype)


def gmlp(u, v, w_s, b_full, *, chunks=4):
    m = u.shape[0]
    tm = chunks * GM_CHUNK
    return pl.pallas_call(
        functools.partial(_gmlp_kernel, chunks=chunks),
        grid=(m // tm,),
        in_specs=[pl.BlockSpec((tm, GM_WIDTH), lambda i: (i, 0)),
                  pl.BlockSpec((tm, GM_WIDTH), lambda i: (i, 0)),
                  pl.BlockSpec((GM_GROUPS, GM_CHUNK, GM_CHUNK), lambda i: (0, 0, 0)),
                  pl.BlockSpec((GM_CHUNK, GM_WIDTH), lambda i: (0, 0))],
        out_specs=pl.BlockSpec((tm, GM_WIDTH), lambda i: (i, 0)),
        out_shape=jax.ShapeDtypeStruct((m, GM_WIDTH), BF16),
        compiler_params=_params("parallel"),
        name="gmlp",
    )(u, v, w_s, b_full)


def _gmlp_first_kernel(u_ref, v_ref, w_ref, b_ref, o_ref):
    o_ref[...] = (u_ref[...].astype(F32) * (w_ref[...] * v_ref[...] + b_ref[...])).astype(o_ref.dtype)


def gmlp_first(u, v, w_s, b_s):
    m = u.shape[0]
    w0 = jnp.repeat(w_s[:, 0, 0], GM_GROUP_DIM).reshape(1, GM_WIDTH)
    b0 = jnp.repeat(b_s[:, 0], GM_GROUP_DIM).reshape(1, GM_WIDTH)
    return pl.pallas_call(
        _gmlp_first_kernel,
        out_shape=jax.ShapeDtypeStruct((m, GM_WIDTH), BF16),
        name="gmlp_first",
    )(u, v, w0, b0)


def _mla_prompt_kernel(q_ref, k_ref, v_ref, o_ref, *, bq):
    i = pl.program_id(2)
    t = lax.broadcasted_iota(jnp.int32, (bq, bq), 0)
    s = lax.broadcasted_iota(jnp.int32, (bq, bq), 1)
    causal = s <= t
    for hh in range(2):
        qk_lanes = slice(hh * MLA_HEAD_PAD, (hh + 1) * MLA_HEAD_PAD)
        v_lanes = slice(hh * MLA_VDIM, (hh + 1) * MLA_VDIM)
        qh = q_ref[:, qk_lanes]

        def scores(j):
            kb = k_ref[pl.ds(j * bq, bq), qk_lanes]
            vb = v_ref[pl.ds(j * bq, bq), v_lanes]
            return _dot_nt(qh, kb), vb

        sc, vb = scores(i)
        sc = jnp.where(causal, sc, NEG_INF)
        m = jnp.max(sc, axis=-1, keepdims=True)
        p = jnp.exp(sc - m)
        l = jnp.sum(p, axis=-1, keepdims=True)
        acc = _dot(p.astype(BF16), vb)

        def body(j, carry):
            m, l, acc = carry
            sc, vb = scores(j)
            m_new = jnp.maximum(m, jnp.max(sc, axis=-1, keepdims=True))
            alpha = jnp.exp(m - m_new)
            p = jnp.exp(sc - m_new)
            return m_new, l * alpha + jnp.sum(p, axis=-1, keepdims=True), acc * alpha + _dot(p.astype(BF16), vb)

        m, l, acc = lax.fori_loop(0, i, body, (m, l, acc))
        o_ref[:, v_lanes] = (acc / l).astype(o_ref.dtype)


def mla_prompt(q, k, v, *, batch, bq=256):
    m = q.shape[0]
    t = m // batch
    nq = t // bq
    return pl.pallas_call(
        functools.partial(_mla_prompt_kernel, bq=bq),
        grid=(batch, MLA_HEADS // 2, nq),
        in_specs=[pl.BlockSpec((bq, 2 * MLA_HEAD_PAD), lambda b, h, i: (b * nq + i, h)),
                  pl.BlockSpec((t, 2 * MLA_HEAD_PAD), lambda b, h, i: (b, h)),
                  pl.BlockSpec((t, 2 * MLA_VDIM), lambda b, h, i: (b, h))],
        out_specs=pl.BlockSpec((bq, 2 * MLA_VDIM), lambda b, h, i: (b * nq + i, h)),
        out_shape=jax.ShapeDtypeStruct((m, MLA_WIDTH), BF16),
        compiler_params=_params("parallel", "parallel", "arbitrary"),
        name="mla_prompt",
    )(q, k, v)


def _mla_decode_kernel(pt_ref, qn_ref, qr_ref, bd_ref, wk_ref, *refs, pages):
    ckv_refs, kr_refs = refs[:pages], refs[pages:2 * pages]
    acc_ref, m_ref, l_ref = refs[2 * pages:2 * pages + 3]
    c = pl.program_id(1)

    @pl.when(c == 0)
    def _():
        acc_ref[...] = jnp.zeros_like(acc_ref)
        m_ref[...] = jnp.full_like(m_ref, NEG_INF)
        l_ref[...] = jnp.zeros_like(l_ref)

    qn, qr, bd, wk = qn_ref[...], qr_ref[...], bd_ref[...], wk_ref[...]
    ones_r = jnp.ones(qr.shape, BF16)
    acc = acc_ref[...]
    m = m_ref[:, 0:1]
    l = l_ref[:, 0:1]
    for p in range(pages):
        ck = ckv_refs[p][...].astype(BF16)
        kr = kr_refs[p][...]
        kn = _dot(ck, wk)
        s_raw = _dot_nt(qn, kn.astype(BF16)) + _dot(qr, kr.astype(BF16))
        ss = _dot_nt(bd, (kn * kn).astype(BF16)) + _dot(ones_r, (kr * kr).astype(BF16))
        sc = s_raw * lax.rsqrt(ss * (1.0 / MLA_QK) + EPS)
        m_new = jnp.maximum(m, jnp.max(sc, axis=-1, keepdims=True))
        alpha = jnp.exp(m - m_new)
        pr = jnp.exp(sc - m_new)
        l = l * alpha + jnp.sum(pr, axis=-1, keepdims=True)
        acc = acc * alpha + _dot(pr.astype(BF16), ck)
        m = m_new
    acc_ref[...] = acc
    m_ref[...] = jnp.broadcast_to(m, m_ref.shape)
    l_ref[...] = jnp.broadcast_to(l, l_ref.shape)


def mla_decode(qn, qr, w_uk, cache_ckv, cache_krope, page_table, layer, *, pages=8):
    b, n_pages = page_table.shape
    steps = n_pages // pages
    nope_w = MLA_HEADS * MLA_NOPE

    def page_spec(*page_shape):
        def make(p):
            return pl.BlockSpec((None, None) + page_shape,
                                lambda s, c, pt: (layer, pt[s * n_pages + c * pages + p], 0, 0))
        return [make(p) for p in range(pages)]

    krope_t = jnp.transpose(cache_krope, (0, 1, 3, 2))

    per_seq = lambda w: pl.BlockSpec((None, MLA_HEADS, w), lambda s, c, pt: (s, 0, 0))
    grid_spec = pltpu.PrefetchScalarGridSpec(
        num_scalar_prefetch=1,
        grid=(b, steps),
        in_specs=[per_seq(nope_w), per_seq(MLA_ROPE),
                  pl.BlockSpec((MLA_HEADS, nope_w), lambda s, c, pt: (0, 0)),
                  pl.BlockSpec((MLA_KV_LORA, nope_w), lambda s, c, pt: (0, 0))]
                 + page_spec(PAGE_SIZE, MLA_KV_LORA) + page_spec(MLA_ROPE, PAGE_SIZE),
        out_specs=[per_seq(MLA_KV_LORA), per_seq(LANES), per_seq(LANES)],
    )
    bd = jnp.asarray(np.arange(nope_w)[None, :] // MLA_NOPE == np.arange(MLA_HEADS)[:, None], BF16)
    return pl.pallas_call(
        functools.partial(_mla_decode_kernel, pages=pages),
        grid_spec=grid_spec,
        out_shape=[jax.ShapeDtypeStruct((b, MLA_HEADS, MLA_KV_LORA), F32),
                   jax.ShapeDtypeStruct((b, MLA_HEADS, LANES), F32),
                   jax.ShapeDtypeStruct((b, MLA_HEADS, LANES), F32)],
        compiler_params=_params("parallel", "arbitrary"),
        name="mla_decode",
    )(page_table.reshape(-1), qn, qr, bd, w_uk, *([cache_ckv] * pages), *([krope_t] * pages))


def _mla_decode_finish_kernel(acc_ref, m_ref, l_ref, q_ref, k_ref, v_ref, wv_ref, o_ref):
    for h in range(MLA_HEADS):
        hl = slice(h * MLA_HEAD_PAD, (h + 1) * MLA_HEAD_PAD)
        vl = slice(h * MLA_VDIM, (h + 1) * MLA_VDIM)
        s_self = jnp.sum(q_ref[:, hl] * k_ref[:, hl], axis=-1, keepdims=True)
        m_past = m_ref[:, h * LANES:h * LANES + 1]
        l_past = l_ref[:, h * LANES:h * LANES + 1]
        m_new = jnp.maximum(m_past, s_self)
        a_past = jnp.exp(m_past - m_new)
        a_self = jnp.exp(s_self - m_new)
        lat = acc_ref[:, h * MLA_KV_LORA:(h + 1) * MLA_KV_LORA].astype(BF16)
        val = _dot(lat, wv_ref[:, vl])
        o_ref[:, vl] = ((val * a_past + v_ref[:, vl] * a_self) / (l_past * a_past + a_self)).astype(o_ref.dtype)


def mla_decode_finish(acc, m, l, q, k_self, v_self, w_uv):
    b = q.shape[0]
    return pl.pallas_call(
        _mla_decode_finish_kernel,
        out_shape=jax.ShapeDtypeStruct((b, MLA_WIDTH), BF16),
        name="mla_decode_finish",
    )(acc.reshape(b, -1), m.reshape(b, -1), l.reshape(b, -1), q, k_self, v_self, w_uv)


def _merge_kernel(x_ref, gl_ref, oa_ref, ob_ref, oc_ref, wa_ref, wb_ref, wc_ref, wo_ref, y_ref):
    merged = None
    for n, (o_ref, w_ref) in enumerate(((oa_ref, wa_ref), (ob_ref, wb_ref), (oc_ref, wc_ref))):
        gate = jax.nn.sigmoid(gl_ref[:, n * D_MODEL:(n + 1) * D_MODEL])
        term = gate * _dot(o_ref[...], w_ref[...])
        merged = term if merged is None else merged + term
    y_ref[...] = x_ref[...] + _dot(merged.astype(BF16), wo_ref[...])


def merge(x, z, o_a, o_b, o_c, w_a, w_b, w_c, w_out, *, tm):
    m = x.shape[0]
    row = lambda w: pl.BlockSpec((tm, w), lambda i: (i, 0))
    wspec = lambda r: pl.BlockSpec((r, D_MODEL), lambda i: (0, 0))
    return pl.pallas_call(
        _merge_kernel,
        grid=(m // tm,),
        in_specs=[row(D_MODEL), pl.BlockSpec((tm, N_BRANCH * D_MODEL), lambda i: (i, 1)),
                  row(SB_WIDTH), row(GM_WIDTH), row(MLA_WIDTH),
                  wspec(SB_WIDTH), wspec(GM_WIDTH), wspec(MLA_WIDTH), wspec(D_MODEL)],
        out_specs=row(D_MODEL),
        out_shape=jax.ShapeDtypeStruct((m, D_MODEL), F32),
        compiler_params=_params("parallel"),
        name="merge",
    )(x, z, o_a, o_b, o_c, w_a, w_b, w_c, w_out)


def _swiglu_tile(h, wg, wu, wd):
    a = _dot(h, wg)
    return _dot((a * jax.nn.sigmoid(a) * _dot(h, wu)).astype(BF16), wd)


def _ffn_kernel(x_ref, g_ref, wg_ref, wu_ref, wd_ref, y_ref, h_ref, acc_ref):
    f = pl.program_id(1)

    @pl.when(f == 0)
    def _():
        x = x_ref[...]
        ms = jnp.mean(x * x, axis=-1, keepdims=True)
        h_ref[...] = (x * lax.rsqrt(ms + EPS) * g_ref[...]).astype(BF16)
        acc_ref[...] = x

    acc_ref[...] += _swiglu_tile(h_ref[...], wg_ref[...], wu_ref[...], wd_ref[...])

    @pl.when(f == pl.num_programs(1) - 1)
    def _():
        y_ref[...] = acc_ref[...]


def ffn(x, g, w_gate, w_up, w_down, *, tm, tf):
    m = x.shape[0]
    d_ff = w_gate.shape[1]
    return pl.pallas_call(
        _ffn_kernel,
        grid=(m // tm, d_ff // tf),
        in_specs=[pl.BlockSpec((tm, D_MODEL), lambda i, f: (i, 0)),
                  pl.BlockSpec((1, D_MODEL), lambda i, f: (0, 0)),
                  pl.BlockSpec((D_MODEL, tf), lambda i, f: (0, f)),
                  pl.BlockSpec((D_MODEL, tf), lambda i, f: (0, f)),
                  pl.BlockSpec((tf, D_MODEL), lambda i, f: (f, 0))],
        out_specs=pl.BlockSpec((tm, D_MODEL), lambda i, f: (i, 0)),
        out_shape=jax.ShapeDtypeStruct((m, D_MODEL), F32),
        scratch_shapes=[pltpu.VMEM((tm, D_MODEL), BF16), pltpu.VMEM((tm, D_MODEL), F32)],
        compiler_params=_params("parallel", "arbitrary"),
        name="ffn",
    )(x, g.reshape(1, -1), w_gate, w_up, w_down)


def _router_kernel(x_ref, g_ref, wr_ref, h_ref, idx_ref, wt_ref):
    x = x_ref[...]
    h = x * lax.rsqrt(jnp.mean(x * x, axis=-1, keepdims=True) + EPS) * g_ref[...]
    h_ref[...] = h
    h1 = h.astype(BF16)
    r = h - h1.astype(F32)
    h2 = r.astype(BF16)
    h3 = (r - h2.astype(F32)).astype(BF16)
    w1, w2, w3 = wr_ref[0], wr_ref[1], wr_ref[2]
    logits = (_dot(h1, w1) + (_dot(h1, w2) + _dot(h2, w1)) + (_dot(h1, w3) + _dot(h2, w2) + _dot(h3, w1)))
    lane = lax.broadcasted_iota(jnp.int32, logits.shape, 1)
    logits = jnp.where(lane < N_EXPERTS, logits, -jnp.inf)
    m1 = jnp.max(logits, axis=-1, keepdims=True)
    i1 = jnp.min(jnp.where(logits == m1, lane, LANES), axis=-1, keepdims=True)
    rest = jnp.where(lane == i1, -jnp.inf, logits)
    m2 = jnp.max(rest, axis=-1, keepdims=True)
    i2 = jnp.min(jnp.where(rest == m2, lane, LANES), axis=-1, keepdims=True)
    e = jnp.exp(m2 - m1)
    p1 = 1.0 / (1.0 + e)
    p2 = e / (1.0 + e)
    idx_ref[...] = jnp.where(lane == 0, i1, jnp.where(lane == 1, i2, 0))
    wt_ref[...] = jnp.where(lane == 0, p1, jnp.where(lane == 1, p2, 0.0))


def router(x, g, w_router, *, tm):
    m = x.shape[0]
    w = jnp.pad(w_router, ((0, 0), (0, LANES - N_EXPERTS)))
    w1 = w.astype(BF16)
    r = w - w1.astype(F32)
    w2 = r.astype(BF16)
    w3 = (r - w2.astype(F32)).astype(BF16)
    row = lambda wd: pl.BlockSpec((tm, wd), lambda i: (i, 0))
    return pl.pallas_call(
        _router_kernel,
        grid=(m // tm,),
        in_specs=[row(D_MODEL), pl.BlockSpec((1, D_MODEL), lambda i: (0, 0)),
                  pl.BlockSpec((3, D_MODEL, LANES), lambda i: (0, 0, 0))],
        out_specs=[row(D_MODEL), row(LANES), row(LANES)],
        out_shape=[jax.ShapeDtypeStruct((m, D_MODEL), F32), jax.ShapeDtypeStruct((m, LANES), jnp.int32),
                   jax.ShapeDtypeStruct((m, LANES), F32)],
        compiler_params=_params("parallel"),
        name="router",
    )(x, g.reshape(1, -1), jnp.stack([w1, w2, w3]))


def _experts_kernel(te_ref, nv_ref, src_ref, h_hbm, wg_ref, wu_ref, wd_ref, y_ref, hbuf, hb16, acc_ref, sem, *, tg):
    i, f = pl.program_id(0), pl.program_id(1)
    valid = i < nv_ref[0]

    def row_copy(r):
        return pltpu.make_async_copy(h_hbm.at[pl.ds(src_ref[0, 0, r], 1), :], hbuf.at[pl.ds(r, 1), :], sem)

    @pl.when(jnp.logical_and(valid, f == 0))
    def _():
        def start(r, c):
            row_copy(r).start()
            return c

        def wait(r, c):
            row_copy(r).wait()
            return c

        lax.fori_loop(0, tg, start, 0)
        lax.fori_loop(0, tg, wait, 0)
        hb16[...] = hbuf[...].astype(BF16)
        acc_ref[...] = jnp.zeros_like(acc_ref)

    @pl.when(valid)
    def _():
        acc_ref[...] += _swiglu_tile(hb16[...], wg_ref[...], wu_ref[...], wd_ref[...])

    @pl.when(f == pl.num_programs(1) - 1)
    def _():
        y_ref[...] = jnp.where(valid, acc_ref[...], 0.0)


def experts(h, src, tile_expert, n_valid, w_gate, w_up, w_down, *, tg, tf):
    n_tiles = src.shape[0]
    d_ff = w_gate.shape[2]
    nf = d_ff // tf

    def w_index(i, f, te, nv):
        ok = i < nv[0]
        return te[i], jnp.where(ok, f, nf - 1)

    grid_spec = pltpu.PrefetchScalarGridSpec(
        num_scalar_prefetch=2,
        grid=(n_tiles, nf),
        in_specs=[pl.BlockSpec((1, 1, tg), lambda i, f, te, nv: (i, 0, 0), memory_space=pltpu.SMEM),
                  pl.BlockSpec(memory_space=pl.ANY),
                  pl.BlockSpec((None, D_MODEL, tf), lambda i, f, te, nv: (w_index(i, f, te, nv)[0], 0, w_index(i, f, te, nv)[1])),
                  pl.BlockSpec((None, D_MODEL, tf), lambda i, f, te, nv: (w_index(i, f, te, nv)[0], 0, w_index(i, f, te, nv)[1])),
                  pl.BlockSpec((None, tf, D_MODEL), lambda i, f, te, nv: (w_index(i, f, te, nv)[0], w_index(i, f, te, nv)[1], 0))],
        out_specs=pl.BlockSpec((tg, D_MODEL), lambda i, f, te, nv: (i, 0)),
        scratch_shapes=[pltpu.VMEM((tg, D_MODEL), F32), pltpu.VMEM((tg, D_MODEL), BF16),
                        pltpu.VMEM((tg, D_MODEL), F32), pltpu.SemaphoreType.DMA(())],
    )
    return pl.pallas_call(
        functools.partial(_experts_kernel, tg=tg),
        grid_spec=grid_spec,
        out_shape=jax.ShapeDtypeStruct((n_tiles * tg, D_MODEL), F32),
        compiler_params=_params("arbitrary", "arbitrary"),
        name="experts",
    )(tile_expert, n_valid, src, h, w_gate, w_up, w_down)


def _combine_kernel(pos_ref, x_ref, wt_ref, y_hbm, o_ref, ybuf, sem, *, tc):
    def row_copy(r, k):
        return pltpu.make_async_copy(y_hbm.at[pl.ds(pos_ref[0, k, r], 1), :], ybuf.at[k, pl.ds(r, 1), :], sem)

    def start(r, c):
        row_copy(r, 0).start()
        row_copy(r, 1).start()
        return c

    def wait(r, c):
        row_copy(r, 0).wait()
        row_copy(r, 1).wait()
        return c

    lax.fori_loop(0, tc, start, 0)
    lax.fori_loop(0, tc, wait, 0)
    wt = wt_ref[...]
    o_ref[...] = x_ref[...] + (wt[:, 0:1] * ybuf[0] + wt[:, 1:2] * ybuf[1])


def combine(x, wt, pos, y, *, tc):
    m = x.shape[0]
    return pl.pallas_call(
        functools.partial(_combine_kernel, tc=tc),
        grid=(m // tc,),
        in_specs=[pl.BlockSpec((1, TOP_K, tc), lambda i: (i, 0, 0), memory_space=pltpu.SMEM),
                  pl.BlockSpec((tc, D_MODEL), lambda i: (i, 0)),
                  pl.BlockSpec((tc, LANES), lambda i: (i, 0)),
                  pl.BlockSpec(memory_space=pl.ANY)],
        out_specs=pl.BlockSpec((tc, D_MODEL), lambda i: (i, 0)),
        out_shape=jax.ShapeDtypeStruct((m, D_MODEL), F32),
        scratch_shapes=[pltpu.VMEM((TOP_K, tc, D_MODEL), F32), pltpu.SemaphoreType.DMA(())],
        compiler_params=_params("arbitrary"),
        name="combine",
    )(pos, x, wt, y)


def _route_plan(idx, *, tg):
    m = idx.shape[0]
    e_flat = idx.reshape(-1)
    onehot = (e_flat[:, None] == jnp.arange(N_EXPERTS)[None, :]).astype(jnp.int32)
    ranks = jnp.cumsum(onehot, axis=0) - onehot
    rank = jnp.sum(ranks * onehot, axis=1)
    counts = jnp.sum(onehot, axis=0)
    tiles = (counts + tg - 1) // tg
    tile_end = jnp.cumsum(tiles)
    start = (tile_end - tiles) * tg
    pos = start[e_flat] + rank
    n_tiles = (TOP_K * m) // tg + N_EXPERTS
    src = jnp.zeros((n_tiles * tg,), jnp.int32).at[pos].set(jnp.arange(TOP_K * m, dtype=jnp.int32) // TOP_K)
    tile_ids = jnp.arange(n_tiles)
    tile_expert = jnp.minimum(jnp.sum(tile_ids[:, None] >= tile_end[None, :], axis=1), N_EXPERTS - 1)
    n_valid = tile_end[-1:]
    last = tile_expert[jnp.maximum(n_valid[0] - 1, 0)]
    tile_expert = jnp.where(tile_ids < n_valid[0], tile_expert, last).astype(jnp.int32)
    return src.reshape(n_tiles, 1, tg), tile_expert, n_valid.astype(jnp.int32), pos.reshape(m, TOP_K)


def moe(x, g, w_router, w_gate, w_up, w_down, *, tm, tg, tf):
    m = x.shape[0]
    h, idx, wt = router(x, g, w_router, tm=tm)
    src, tile_expert, n_valid, pos = _route_plan(idx[:, :TOP_K], tg=tg)
    y = experts(h, src, tile_expert, n_valid, w_gate, w_up, w_down, tg=tg, tf=tf)
    pos_t = pos.reshape(m // tm, tm, TOP_K).transpose(0, 2, 1)
    return combine(x, wt, pos_t, y, tc=tm)


def _rope_tables(pos):
    t = pos.shape[0]
    inv_freq = ROPE_THETA ** (-jnp.arange(0, MLA_ROPE, 2, dtype=F32) / MLA_ROPE)
    ang = pos[:, None] * inv_freq[None, :]
    c, s = jnp.cos(ang), jnp.sin(ang)
    half = MLA_ROPE // 2
    zeros = lambda w: jnp.zeros((t, w), F32)
    tail = LANES - MLA_QK
    return (jnp.concatenate([jnp.ones((t, MLA_NOPE), F32), c, c, zeros(tail)], axis=1),
            jnp.concatenate([zeros(MLA_NOPE), -s, zeros(half), zeros(tail)], axis=1),
            jnp.concatenate([zeros(MLA_NOPE), zeros(half), s, zeros(tail)], axis=1))


def _pad_heads(w, used):
    rows, heads, _ = w.shape
    return jnp.pad(w, ((0, 0), (0, 0), (0, MLA_HEAD_PAD - used))).reshape(rows, heads * MLA_HEAD_PAD)


def _layer_weights(l, w):
    d = {}
    cols = jnp.split(w['w_in'][l], np.cumsum([SB_WIDTH, SB_KV_WIDTH, SB_KV_WIDTH, GM_WIDTH, GM_WIDTH, MLA_Q_LORA,
                                              MLA_KV_LORA, MLA_ROPE]).tolist(), axis=1)
    zc = lambda n: jnp.zeros((D_MODEL, n), F32)
    krope = jnp.concatenate([zc(MLA_NOPE), cols[7], zc(LANES - MLA_QK)], axis=1)
    used = Z_KROPE + LANES
    d['w_in'] = jnp.concatenate(cols[:7] + [krope, zc(Z_MIX - used), cols[8]], axis=1).astype(BF16)
    d['g_mix'] = w['g_mix'][l]
    d['g_sb_q'] = (jnp.tile(w['g_sb_q'][l], SB_HEADS) * HEAD_DIM ** -0.5).reshape(1, -1)
    d['g_sb_k'] = jnp.tile(w['g_sb_k'][l], SB_KV_HEADS).reshape(1, -1)
    d['g_gm_v'] = w['g_gm_v'][l].reshape(1, -1)
    d['g_mla_cq'] = w['g_mla_cq'][l].reshape(1, -1)
    d['g_mla_ckv'] = w['g_mla_ckv'][l].reshape(1, -1)
    d['w_uq'] = _pad_heads(w['w_mla_uq'][l], MLA_QK).astype(BF16)
    d['w_uk_pad'] = _pad_heads(w['w_mla_uk'][l], MLA_NOPE).astype(BF16)
    d['w_uk'] = w['w_mla_uk'][l].reshape(MLA_KV_LORA, -1).astype(BF16)
    d['w_uv'] = w['w_mla_uv'][l].reshape(MLA_KV_LORA, -1).astype(BF16)
    rot_gain = jnp.concatenate([jnp.ones((MLA_ROPE,), F32), jnp.zeros((LANES - MLA_QK,), F32)])
    d['gain_q'] = (jnp.tile(jnp.concatenate([w['g_mla_q'][l], rot_gain]), MLA_HEADS) * MLA_QK ** -0.5).reshape(1, -1)
    d['gain_k'] = jnp.tile(jnp.concatenate([w['g_mla_k'][l], rot_gain]), MLA_HEADS).reshape(1, -1)
    d['g_mla_k'] = w['g_mla_k'][l]
    d['w_gm_s'] = w['w_gm_s'][l]
    d['b_gm_s'] = w['b_gm_s'][l]
    d['b_gm_full'] = jnp.repeat(w['b_gm_s'][l].T, GM_GROUP_DIM, axis=1)
    for name in ('w_br_a', 'w_br_b', 'w_br_c', 'w_out'):
        d[name] = w[name][l].astype(BF16)
    d['g_ffn'] = w['g_ffn'][l]
    i = l // 2
    if l % 2 == 0:
        for name in ('w_ffn_gate', 'w_ffn_up', 'w_ffn_down'):
            d[name] = w[name][i].astype(BF16)
    else:
        d['w_router'] = w['w_router'][i]
        for name in ('w_exp_gate', 'w_exp_up', 'w_exp_down'):
            d[name] = w[name][i].astype(BF16)
    return d


def _mixer_inputs(x, d, tabs, *, tm, qk_dtype):
    z = norm_matmul(x, d['g_mix'], d['w_in'], tm=tm, tn=1024)
    q_a, k_a, v_a, u_b, v_b, cqn, ckv, krot = prep(z, d['g_sb_q'], d['g_sb_k'], d['g_gm_v'], d['g_mla_cq'],
                                                   d['g_mla_ckv'], tabs, tm=tm)
    q_c = mla_q(cqn, d['w_uq'], d['gain_q'], tabs, tm=tm, out_dtype=qk_dtype)
    k_c, v_c = mla_kv(ckv, krot, d['w_uk_pad'], d['w_uv'], d['gain_k'], tm=tm, out_dtype=qk_dtype)
    return z, q_a, k_a, v_a, u_b, v_b, q_c, k_c, v_c, ckv, krot


def _channel_mixer(x, l, d, *, tm, tg):
    if l % 2 == 0:
        return ffn(x, d['g_ffn'], d['w_ffn_gate'], d['w_ffn_up'], d['w_ffn_down'], tm=tm, tf=d['w_ffn_gate'].shape[1] // 2)
    return moe(x, d['g_ffn'], d['w_router'], d['w_exp_gate'], d['w_exp_up'], d['w_exp_down'], tm=tm, tg=tg,
               tf=d['w_exp_gate'].shape[2] // 4)


def _prompt_layer(x, l, d, tabs, batch):
    tm = 512
    z, q_a, k_a, v_a, u_b, v_b, q_c, k_c, v_c, ckv, krot = _mixer_inputs(x, d, tabs, tm=tm, qk_dtype=BF16)
    o_a = sb_prompt(q_a, k_a, v_a, batch=batch)
    o_b = gmlp(u_b, v_b, d['w_gm_s'], d['b_gm_full'])
    o_c = mla_prompt(q_c, k_c, v_c, batch=batch)
    x = merge(x, z, o_a, o_b, o_c, d['w_br_a'], d['w_br_b'], d['w_br_c'], d['w_out'], tm=tm)
    x = _channel_mixer(x, l, d, tm=tm, tg=512)
    return x, (k_a, v_a, ckv, krot[:, MLA_NOPE:MLA_QK])


def _sample_layer(x, l, d, tabs, caches, page_table):
    tm = x.shape[0]
    cache_sb_k, cache_sb_v, cache_ckv, cache_krope = caches
    z, q_a, k_a, v_a, u_b, v_b, q_c, k_c, v_c, ckv, krot = _mixer_inputs(x, d, tabs, tm=tm, qk_dtype=F32)
    kv_of_head = np.arange(SB_HEADS) // SB_GROUP
    lane_kv = np.arange(SB_KV_WIDTH) // HEAD_DIM
    head_mask = jnp.asarray(kv_of_head[:, None] == lane_kv[None, :], q_a.dtype)
    q_bd = jnp.tile(q_a.reshape(tm, SB_HEADS, 1, HEAD_DIM), (1, 1, SB_KV_HEADS, 1)).reshape(tm, SB_HEADS, SB_KV_WIDTH)
    o_full = sb_decode(q_bd * head_mask, cache_sb_k, cache_sb_v, page_table, l)
    o_full = o_full.reshape(tm, SB_KV_HEADS, SB_GROUP, SB_KV_HEADS, HEAD_DIM)
    o_a = jnp.stack([o_full[:, h, :, h, :] for h in range(SB_KV_HEADS)], axis=1).reshape(tm, SB_WIDTH).astype(BF16)
    o_b = gmlp_first(u_b, v_b, d['w_gm_s'], d['b_gm_s'])
    q_heads = q_c.reshape(tm, MLA_HEADS, MLA_HEAD_PAD)
    eye = jnp.eye(MLA_HEADS, dtype=F32)
    qn = (q_heads[:, :, :MLA_NOPE] * d['g_mla_k'])[:, :, None, :] * eye[None, :, :, None]
    qn = qn.reshape(tm, MLA_HEADS, MLA_HEADS * MLA_NOPE).astype(BF16)
    qr = q_heads[:, :, MLA_NOPE:MLA_QK].astype(BF16)
    acc, m_run, l_run = mla_decode(qn, qr, d['w_uk'], cache_ckv, cache_krope, page_table, l)
    o_c = mla_decode_finish(acc, m_run, l_run, q_c, k_c, v_c, d['w_uv'])
    x = merge(x, z, o_a, o_b, o_c, d['w_br_a'], d['w_br_b'], d['w_br_c'], d['w_out'], tm=tm)
    x = _channel_mixer(x, l, d, tm=tm, tg=128)
    return x, (k_a, v_a, ckv, krot[:, MLA_NOPE:MLA_QK], v_b)


def kernel(x_prompt, x_sample, cache_sb_k, cache_sb_v, cache_mla_ckv, cache_mla_krope, page_table, g_mix, w_in, g_sb_q, g_sb_k, g_gm_v, w_gm_s, b_gm_s, g_mla_cq, g_mla_ckv, w_mla_uq, w_mla_uk, w_mla_uv, g_mla_q, g_mla_k, w_br_a, w_br_b, w_br_c, w_out, g_ffn, w_ffn_gate, w_ffn_up, w_ffn_down, w_router, w_exp_gate, w_exp_up, w_exp_down):
    weights = dict(g_mix=g_mix, w_in=w_in, g_sb_q=g_sb_q, g_sb_k=g_sb_k, g_gm_v=g_gm_v, w_gm_s=w_gm_s, b_gm_s=b_gm_s,
                   g_mla_cq=g_mla_cq, g_mla_ckv=g_mla_ckv, w_mla_uq=w_mla_uq, w_mla_uk=w_mla_uk, w_mla_uv=w_mla_uv,
                   g_mla_q=g_mla_q, g_mla_k=g_mla_k, w_br_a=w_br_a, w_br_b=w_br_b, w_br_c=w_br_c, w_out=w_out,
                   g_ffn=g_ffn, w_ffn_gate=w_ffn_gate, w_ffn_up=w_ffn_up, w_ffn_down=w_ffn_down, w_router=w_router,
                   w_exp_gate=w_exp_gate, w_exp_up=w_exp_up, w_exp_down=w_exp_down)
    batch, seq, _ = x_prompt.shape
    dec_batch, dec_seq, _ = x_sample.shape
    depth = g_mix.shape[0]
    past_len = page_table.shape[1] * PAGE_SIZE
    assert dec_seq == 1, "the sample path handles one new token per sequence"
    tabs_p = _rope_tables(jnp.arange(seq, dtype=F32))
    tabs_s = tuple(jnp.broadcast_to(t, (dec_batch, LANES))
                   for t in _rope_tables(jnp.arange(dec_seq, dtype=F32) + past_len))
    caches = (cache_sb_k, cache_sb_v, cache_mla_ckv, cache_mla_krope)
    h_p = x_prompt.reshape(batch * seq, D_MODEL)
    h_s = x_sample.reshape(dec_batch * dec_seq, D_MODEL)
    outs_p, outs_s = [], []
    for l in range(depth):
        d = _layer_weights(l, weights)
        h_p, new_p = _prompt_layer(h_p, l, d, tabs_p, batch)
        h_s, new_s = _sample_layer(h_s, l, d, tabs_s, caches, page_table)
        outs_p.append(new_p)
        outs_s.append(new_s)

    def stack(outs, n, shape):
        return jnp.stack([o[n].reshape(shape) for o in outs])

    kv_p = (batch, seq, SB_KV_HEADS, HEAD_DIM)
    kv_s = (dec_batch, dec_seq, SB_KV_HEADS, HEAD_DIM)
    return (h_p.reshape(batch, seq, D_MODEL), h_s.reshape(dec_batch, dec_seq, D_MODEL),
            stack(outs_p, 0, kv_p), stack(outs_p, 1, kv_p),
            stack(outs_p, 2, (batch, seq, MLA_KV_LORA)), stack(outs_p, 3, (batch, seq, MLA_ROPE)),
            stack(outs_s, 0, kv_s), stack(outs_s, 1, kv_s),
            stack(outs_s, 2, (dec_batch, dec_seq, MLA_KV_LORA)), stack(outs_s, 3, (dec_batch, dec_seq, MLA_ROPE)),
            stack(outs_s, 4, (dec_batch, dec_seq, GM_WIDTH)))
```

```python
import functools

import numpy as np
import jax
import jax.numpy as jnp
from jax import lax
from jax.experimental import pallas as pl
from jax.experimental.pallas import tpu as pltpu

F32 = jnp.float32
BF16 = jnp.bfloat16
EPS = 1e-6
NEG_INF = -1e30
LANES = 128

D_MODEL = 1024
HEAD_DIM = 64
SB_HEADS = 8
SB_KV_HEADS = 2
SB_GROUP = SB_HEADS // SB_KV_HEADS
SB_WIDTH = SB_HEADS * HEAD_DIM
SB_KV_WIDTH = SB_KV_HEADS * HEAD_DIM
GM_GROUPS = 8
GM_GROUP_DIM = 64
GM_CHUNK = 128
GM_WIDTH = GM_GROUPS * GM_GROUP_DIM
MLA_HEADS = 8
MLA_NOPE = 64
MLA_ROPE = 32
MLA_QK = MLA_NOPE + MLA_ROPE
MLA_VDIM = 64
MLA_Q_LORA = 768
MLA_KV_LORA = 256
MLA_WIDTH = MLA_HEADS * MLA_VDIM
MLA_HEAD_PAD = LANES
ROPE_THETA = 10000.0
N_BRANCH = 3
N_EXPERTS = 8
TOP_K = 2
PAGE_SIZE = 128

Z_SB_Q = 0
Z_SB_K = Z_SB_Q + SB_WIDTH
Z_SB_V = Z_SB_K + SB_KV_WIDTH
Z_GM_U = Z_SB_V + SB_KV_WIDTH
Z_GM_V = Z_GM_U + GM_WIDTH
Z_CQ = Z_GM_V + GM_WIDTH
Z_CKV = Z_CQ + MLA_Q_LORA
Z_KROPE = Z_CKV + MLA_KV_LORA
Z_MIX = 3 * D_MODEL
Z_COLS = Z_MIX + N_BRANCH * D_MODEL

VMEM_LIMIT = 56 * 1024 * 1024


def _params(*sem, vmem=VMEM_LIMIT):
    return pltpu.CompilerParams(dimension_semantics=sem, vmem_limit_bytes=vmem)


def _dot(a, b):
    return jnp.dot(a, b, preferred_element_type=F32)


def _dot_nt(a, b):
    return lax.dot_general(a, b, (((1,), (1,)), ((), ())), preferred_element_type=F32)


def _split_bf16(x):
    hi = x.astype(BF16)
    lo = (x - hi.astype(F32)).astype(BF16)
    return hi, lo


def _norm_matmul_kernel(x_ref, g_ref, w_ref, o_ref, h_ref):
    @pl.when(pl.program_id(1) == 0)
    def _():
        x = x_ref[...]
        ms = jnp.mean(x * x, axis=-1, keepdims=True)
        h_ref[...] = (x * lax.rsqrt(ms + EPS) * g_ref[...]).astype(BF16)

    o_ref[...] = _dot(h_ref[...], w_ref[...])


def norm_matmul(x, g, w, *, tm, tn):
    m, k = x.shape
    n = w.shape[1]
    return pl.pallas_call(
        _norm_matmul_kernel,
        grid=(m // tm, n // tn),
        in_specs=[pl.BlockSpec((tm, k), lambda i, j: (i, 0)),
                  pl.BlockSpec((1, k), lambda i, j: (0, 0)),
                  pl.BlockSpec((k, tn), lambda i, j: (0, j))],
        out_specs=pl.BlockSpec((tm, tn), lambda i, j: (i, j)),
        out_shape=jax.ShapeDtypeStruct((m, n), F32),
        scratch_shapes=[pltpu.VMEM((tm, k), BF16)],
        compiler_params=_params("parallel", "arbitrary"),
        name="norm_matmul",
    )(x, g.reshape(1, k), w)


def _gelu(x):
    c = np.float32(np.sqrt(2.0 / np.pi))
    return x * (0.5 * (1.0 + jnp.tanh(c * (x + 0.044715 * (x * x * x)))))


def _rope(x, c, s1, s2, width):
    return x * c + pltpu.roll(x, width - MLA_ROPE // 2, 1) * s1 + pltpu.roll(x, MLA_ROPE // 2, 1) * s2


def _group_sumsq(x, bd):
    hi, lo = _split_bf16(x * x)
    return _dot(hi, bd) + _dot(lo, bd)


def _prep_kernel(z_ref, gq_ref, gk_ref, gv_ref, gcq_ref, gckv_ref, c_ref, s1_ref, s2_ref, bdq_ref, bdk_ref,
                 qa_ref, ka_ref, va_ref, ub_ref, vb_ref, cqn_ref, ckv_ref, krot_ref):
    inv_d = 1.0 / HEAD_DIM
    q = z_ref[:, Z_SB_Q:Z_SB_Q + SB_WIDTH]
    qa_ref[...] = (q * lax.rsqrt(_group_sumsq(q, bdq_ref[...]) * inv_d + EPS) * gq_ref[...]).astype(qa_ref.dtype)
    k = z_ref[:, Z_SB_K:Z_SB_K + SB_KV_WIDTH]
    ka_ref[...] = k * lax.rsqrt(_group_sumsq(k, bdk_ref[...]) * inv_d + EPS) * gk_ref[...]
    va_ref[...] = z_ref[:, Z_SB_V:Z_SB_V + SB_KV_WIDTH]
    ub_ref[...] = _gelu(z_ref[:, Z_GM_U:Z_GM_U + GM_WIDTH]).astype(ub_ref.dtype)
    gv = _gelu(z_ref[:, Z_GM_V:Z_GM_V + GM_WIDTH])
    vb_ref[...] = gv * lax.rsqrt(jnp.mean(gv * gv, axis=-1, keepdims=True) + EPS) * gv_ref[...]
    cq = z_ref[:, Z_CQ:Z_CQ + MLA_Q_LORA]
    cqn_ref[...] = (cq * lax.rsqrt(jnp.mean(cq * cq, axis=-1, keepdims=True) + EPS) * gcq_ref[...]).astype(cqn_ref.dtype)
    ckv = z_ref[:, Z_CKV:Z_CKV + MLA_KV_LORA]
    ckv_ref[...] = ckv * lax.rsqrt(jnp.mean(ckv * ckv, axis=-1, keepdims=True) + EPS) * gckv_ref[...]
    kr = z_ref[:, Z_KROPE:Z_KROPE + LANES]
    krot_ref[...] = _rope(kr, c_ref[...], s1_ref[...], s2_ref[...], LANES)


def _block_diag_ones(width, group):
    idx = np.arange(width) // group
    return jnp.asarray(idx[:, None] == idx[None, :], BF16)


def prep(z, gq, gk, gv, gcq, gckv, tabs, *, tm):
    m = z.shape[0]
    nt = tabs[0].shape[0] // tm
    row = lambda w: pl.BlockSpec((tm, w), lambda i: (i, 0))
    vec = lambda w: pl.BlockSpec((1, w), lambda i: (0, 0))
    tab = pl.BlockSpec((tm, LANES), lambda i: (i % nt, 0))
    full = lambda w: pl.BlockSpec((w, w), lambda i: (0, 0))
    outs = [(SB_WIDTH, BF16), (SB_KV_WIDTH, F32), (SB_KV_WIDTH, F32), (GM_WIDTH, BF16), (GM_WIDTH, F32),
            (MLA_Q_LORA, BF16), (MLA_KV_LORA, F32), (LANES, F32)]
    return pl.pallas_call(
        _prep_kernel,
        grid=(m // tm,),
        in_specs=[pl.BlockSpec((tm, Z_MIX), lambda i: (i, 0)),
                  vec(SB_WIDTH), vec(SB_KV_WIDTH), vec(GM_WIDTH), vec(MLA_Q_LORA), vec(MLA_KV_LORA),
                  tab, tab, tab, full(SB_WIDTH), full(SB_KV_WIDTH)],
        out_specs=[row(w) for w, _ in outs],
        out_shape=[jax.ShapeDtypeStruct((m, w), dt) for w, dt in outs],
        compiler_params=_params("parallel"),
        name="prep",
    )(z, gq, gk, gv, gcq, gckv, *tabs, _block_diag_ones(SB_WIDTH, HEAD_DIM), _block_diag_ones(SB_KV_WIDTH, HEAD_DIM))


def _head_norm(xh, gain):
    ss = jnp.sum(xh * xh, axis=-1, keepdims=True) * (1.0 / MLA_QK)
    return xh * lax.rsqrt(ss + EPS) * gain


def _mla_q_kernel(cq_ref, w_ref, gain_ref, c_ref, s1_ref, s2_ref, o_ref):
    q = _dot(cq_ref[...], w_ref[...])
    c, s1, s2 = c_ref[...], s1_ref[...], s2_ref[...]
    for h in range(MLA_HEADS):
        sl = slice(h * MLA_HEAD_PAD, (h + 1) * MLA_HEAD_PAD)
        xh = _rope(q[:, sl], c, s1, s2, LANES)
        o_ref[:, sl] = _head_norm(xh, gain_ref[:, sl]).astype(o_ref.dtype)


def mla_q(cqn, w_uq_pad, gain, tabs, *, tm, out_dtype):
    m = cqn.shape[0]
    nt = tabs[0].shape[0] // tm
    width = MLA_HEADS * MLA_HEAD_PAD
    tab = pl.BlockSpec((tm, LANES), lambda i: (i % nt, 0))
    return pl.pallas_call(
        _mla_q_kernel,
        grid=(m // tm,),
        in_specs=[pl.BlockSpec((tm, MLA_Q_LORA), lambda i: (i, 0)),
                  pl.BlockSpec((MLA_Q_LORA, width), lambda i: (0, 0)),
                  pl.BlockSpec((1, width), lambda i: (0, 0)), tab, tab, tab],
        out_specs=pl.BlockSpec((tm, width), lambda i: (i, 0)),
        out_shape=jax.ShapeDtypeStruct((m, width), out_dtype),
        compiler_params=_params("parallel"),
        name="mla_q",
    )(cqn, w_uq_pad, gain, *tabs)


def _mla_kv_kernel(ckv_ref, krot_ref, wk_ref, wv_ref, gain_ref, k_ref, v_ref):
    ckv = ckv_ref[...].astype(BF16)
    kn = _dot(ckv, wk_ref[...])
    krot = krot_ref[...]
    for h in range(MLA_HEADS):
        sl = slice(h * MLA_HEAD_PAD, (h + 1) * MLA_HEAD_PAD)
        k_ref[:, sl] = _head_norm(kn[:, sl] + krot, gain_ref[:, sl]).astype(k_ref.dtype)
    v_ref[...] = _dot(ckv, wv_ref[...]).astype(v_ref.dtype)


def mla_kv(ckv, krot, w_uk_pad, w_uv, gain, *, tm, out_dtype):
    m = ckv.shape[0]
    width = MLA_HEADS * MLA_HEAD_PAD
    return pl.pallas_call(
        _mla_kv_kernel,
        grid=(m // tm,),
        in_specs=[pl.BlockSpec((tm, MLA_KV_LORA), lambda i: (i, 0)),
                  pl.BlockSpec((tm, LANES), lambda i: (i, 0)),
                  pl.BlockSpec((MLA_KV_LORA, width), lambda i: (0, 0)),
                  pl.BlockSpec((MLA_KV_LORA, MLA_WIDTH), lambda i: (0, 0)),
                  pl.BlockSpec((1, width), lambda i: (0, 0))],
        out_specs=[pl.BlockSpec((tm, width), lambda i: (i, 0)), pl.BlockSpec((tm, MLA_WIDTH), lambda i: (i, 0))],
        out_shape=[jax.ShapeDtypeStruct((m, width), out_dtype), jax.ShapeDtypeStruct((m, MLA_WIDTH), out_dtype)],
        compiler_params=_params("parallel"),
        name="mla_kv",
    )(ckv, krot, w_uk_pad, w_uv, gain)


KEEP_FLOOR = -104.0


def _sb_block(z, mask, keep, u):
    l1p = jnp.log1p(jnp.exp(-jnp.abs(z)))
    log_beta = jnp.minimum(z, 0.0) - l1p
    log_keep = jnp.minimum(-z, 0.0) - l1p
    if mask is not None:
        log_keep = jnp.where(mask, log_keep, 0.0)
    hi, lo = _split_bf16(log_keep)
    after = _dot(hi, u) + _dot(lo, u)
    w = jnp.exp(log_beta + after + keep)
    if mask is not None:
        w = jnp.where(mask, w, 0.0)
    return w, keep + (after[:, 0:1] + log_keep[:, 0:1])


def _later_keys_ones(n):
    return jnp.asarray(np.arange(n)[:, None] > np.arange(n)[None, :], BF16)


def _sb_prompt_kernel(q_ref, k_ref, v_ref, u_ref, o_ref, *, bq):
    i = pl.program_id(1)
    u = u_ref[...]
    rows = SB_GROUP * bq
    r = lax.broadcasted_iota(jnp.int32, (rows, bq), 0) & (bq - 1)
    s = lax.broadcasted_iota(jnp.int32, (rows, bq), 1)
    diag_mask = s < r
    for h in range(SB_KV_HEADS):
        lanes = slice(h * HEAD_DIM, (h + 1) * HEAD_DIM)
        qh = jnp.concatenate(
            [q_ref[:, (h * SB_GROUP + g) * HEAD_DIM:(h * SB_GROUP + g + 1) * HEAD_DIM] for g in range(SB_GROUP)], axis=0)

        def block(j, mask, keep):
            kb = k_ref[pl.ds(j * bq, bq), lanes].astype(BF16)
            vb = v_ref[pl.ds(j * bq, bq), lanes].astype(BF16)
            w, keep = _sb_block(_dot_nt(qh, kb), mask, keep, u)
            return _dot(w.astype(BF16), vb), keep

        acc, keep = block(i, diag_mask, jnp.zeros((rows, 1), F32))

        def more(carry):
            jj, _, keep = carry
            return jnp.logical_and(jj <= i, jnp.max(keep) > KEEP_FLOOR)

        def body(carry):
            jj, acc, keep = carry
            pv, keep = block(i - jj, None, keep)
            return jj + 1, acc + pv, keep

        _, acc, _ = lax.while_loop(more, body, (jnp.int32(1), acc, keep))
        for g in range(SB_GROUP):
            col = (h * SB_GROUP + g) * HEAD_DIM
            o_ref[:, col:col + HEAD_DIM] = acc[g * bq:(g + 1) * bq].astype(o_ref.dtype)


def sb_prompt(q, k, v, *, batch, bq=128):
    m = q.shape[0]
    t = m // batch
    nq = t // bq
    return pl.pallas_call(
        functools.partial(_sb_prompt_kernel, bq=bq),
        grid=(batch, nq),
        in_specs=[pl.BlockSpec((bq, SB_WIDTH), lambda b, i: (b * nq + i, 0)),
                  pl.BlockSpec((t, SB_KV_WIDTH), lambda b, i: (b, 0)),
                  pl.BlockSpec((t, SB_KV_WIDTH), lambda b, i: (b, 0)),
                  pl.BlockSpec((bq, bq), lambda b, i: (0, 0))],
        out_specs=pl.BlockSpec((bq, SB_WIDTH), lambda b, i: (b * nq + i, 0)),
        out_shape=jax.ShapeDtypeStruct((m, SB_WIDTH), BF16),
        compiler_params=_params("parallel", "arbitrary"),
        name="sb_prompt",
    )(q, k, v, _later_keys_ones(bq))


def _sb_decode_kernel(pt_ref, q_ref, u_ref, k_hbm, v_hbm, o_ref, kbuf, vbuf, sem, *, layer, n_pages, pages):
    s = pl.program_id(0)
    slot = s % 2

    def copies(seq, chunk, slot):
        out = []
        for p in range(pages):
            page = pt_ref[seq * n_pages + n_pages - 1 - (chunk * pages + p)]
            out.append(pltpu.make_async_copy(k_hbm.at[layer, page], kbuf.at[slot, p], sem.at[slot, 0]))
            out.append(pltpu.make_async_copy(v_hbm.at[layer, page], vbuf.at[slot, p], sem.at[slot, 1]))
        return out

    def fetch(seq, chunk, slot):
        for c in copies(seq, chunk, slot):
            c.start()

    def arrive(seq, chunk, slot):
        for c in copies(seq, chunk, slot):
            c.wait()

    @pl.when(s == 0)
    def _():
        fetch(s, 0, slot)

    @pl.when(s + 1 < pl.num_programs(0))
    def _():
        fetch(s + 1, 0, 1 - slot)

    q = q_ref[...]
    u = u_ref[...]

    def attend(acc, keep):
        for p in range(pages):
            w, keep = _sb_block(_dot(q, kbuf[slot, p].astype(BF16)), None, keep, u)
            acc = acc + _dot_nt(w.astype(BF16), vbuf[slot, p].astype(BF16))
        return acc, keep

    arrive(s, 0, slot)
    acc, keep = attend(jnp.zeros(o_ref.shape, F32), jnp.zeros((SB_HEADS, 1), F32))

    def more(carry):
        chunk, _, keep = carry
        return jnp.logical_and(chunk < n_pages // pages, jnp.max(keep) > KEEP_FLOOR)

    def body(carry):
        chunk, acc, keep = carry
        fetch(s, chunk, slot)
        arrive(s, chunk, slot)
        acc, keep = attend(acc, keep)
        return chunk + 1, acc, keep

    _, acc, _ = lax.while_loop(more, body, (jnp.int32(1), acc, keep))
    o_ref[...] = acc


def sb_feature_major(cache):
    return jnp.transpose(cache, (0, 1, 3, 4, 2)).reshape(cache.shape[0], cache.shape[1], SB_KV_WIDTH, PAGE_SIZE)


def sb_decode(q_bd, cache_k_t, cache_v_t, page_table, layer, *, pages=2):
    b, n_pages = page_table.shape
    grid_spec = pltpu.PrefetchScalarGridSpec(
        num_scalar_prefetch=1,
        grid=(b,),
        in_specs=[pl.BlockSpec((None, SB_HEADS, SB_KV_WIDTH), lambda s, pt: (s, 0, 0)),
                  pl.BlockSpec((PAGE_SIZE, PAGE_SIZE), lambda s, pt: (0, 0)),
                  pl.BlockSpec(memory_space=pl.ANY), pl.BlockSpec(memory_space=pl.ANY)],
        out_specs=pl.BlockSpec((None, SB_HEADS, SB_KV_WIDTH), lambda s, pt: (s, 0, 0)),
        scratch_shapes=[pltpu.VMEM((2, pages, SB_KV_WIDTH, PAGE_SIZE), F32),
                        pltpu.VMEM((2, pages, SB_KV_WIDTH, PAGE_SIZE), F32),
                        pltpu.SemaphoreType.DMA((2, 2))],
    )
    return pl.pallas_call(
        functools.partial(_sb_decode_kernel, layer=layer, n_pages=n_pages, pages=pages),
        grid_spec=grid_spec,
        out_shape=jax.ShapeDtypeStruct((b, SB_HEADS, SB_KV_WIDTH), F32),
        compiler_params=_params("arbitrary"),
        name="sb_decode",
    )(page_table.reshape(-1), q_bd, _later_keys_ones(PAGE_SIZE), cache_k_t, cache_v_t)


def _gmlp_kernel(u_ref, v_ref, w_ref, b_ref, o_ref, *, chunks):
    t = lax.broadcasted_iota(jnp.int32, (GM_CHUNK, GM_CHUNK), 0)
    s = lax.broadcasted_iota(jnp.int32, (GM_CHUNK, GM_CHUNK), 1)
    causal = s <= t
    group = lax.broadcasted_iota(jnp.int32, (GM_CHUNK, GM_WIDTH), 1) // GM_GROUP_DIM
    ws = [jnp.where(causal, w_ref[g], 0.0).astype(BF16) for g in range(GM_GROUPS)]
    for c in range(chunks):
        rows = slice(c * GM_CHUNK, (c + 1) * GM_CHUNK)
        vc = v_ref[rows, :].astype(BF16)
        mixed = b_ref[...]
        for g in range(GM_GROUPS):
            mixed = mixed + jnp.where(group == g, _dot(ws[g], vc), 0.0)
        o_ref[rows, :] = (u_ref[rows, :].astype(F32) * mixed).astype(o_ref.dtype)


def gmlp(u, v, w_s, b_full, *, chunks=4):
    m = u.shape[0]
    tm = chunks * GM_CHUNK
    return pl.pallas_call(
        functools.partial(_gmlp_kernel, chunks=chunks),
        grid=(m // tm,),
        in_specs=[pl.BlockSpec((tm, GM_WIDTH), lambda i: (i, 0)),
                  pl.BlockSpec((tm, GM_WIDTH), lambda i: (i, 0)),
                  pl.BlockSpec((GM_GROUPS, GM_CHUNK, GM_CHUNK), lambda i: (0, 0, 0)),
                  pl.BlockSpec((GM_CHUNK, GM_WIDTH), lambda i: (0, 0))],
        out_specs=pl.BlockSpec((tm, GM_WIDTH), lambda i: (i, 0)),
        out_shape=jax.ShapeDtypeStruct((m, GM_WIDTH), BF16),
        compiler_params=_params("parallel"),
        name="gmlp",
    )(u, v, w_s, b_full)


def _gmlp_first_kernel(u_ref, v_ref, w_ref, b_ref, o_ref):
    o_ref[...] = (u_ref[...].astype(F32) * (w_ref[...] * v_ref[...] + b_ref[...])).astype(o_ref.dtype)


def gmlp_first(u, v, w_s, b_s):
    m = u.shape[0]
    w0 = jnp.repeat(w_s[:, 0, 0], GM_GROUP_DIM).reshape(1, GM_WIDTH)
    b0 = jnp.repeat(b_s[:, 0], GM_GROUP_DIM).reshape(1, GM_WIDTH)
    return pl.pallas_call(
        _gmlp_first_kernel,
        out_shape=jax.ShapeDtypeStruct((m, GM_WIDTH), BF16),
        name="gmlp_first",
    )(u, v, w0, b0)


def _mla_prompt_kernel(q_ref, k_ref, v_ref, o_ref, *, bq):
    i = pl.program_id(2)
    t = lax.broadcasted_iota(jnp.int32, (bq, bq), 0)
    s = lax.broadcasted_iota(jnp.int32, (bq, bq), 1)
    causal = s <= t
    heads = range(2)
    qk_lanes = [slice(hh * MLA_HEAD_PAD, (hh + 1) * MLA_HEAD_PAD) for hh in heads]
    v_lanes = [slice(hh * MLA_VDIM, (hh + 1) * MLA_VDIM) for hh in heads]

    def key_block(j, carry, mask):
        out = []
        for hh in heads:
            m, l, acc = carry[hh]
            sc = _dot_nt(q_ref[:, qk_lanes[hh]], k_ref[pl.ds(j * bq, bq), qk_lanes[hh]])
            if mask is not None:
                sc = jnp.where(mask, sc, NEG_INF)
            m_new = jnp.maximum(m, jnp.max(sc, axis=-1, keepdims=True))
            alpha = jnp.exp(m - m_new)
            p = jnp.exp(sc - m_new)
            pv = _dot(p.astype(BF16), v_ref[pl.ds(j * bq, bq), v_lanes[hh]])
            out.append((m_new, l * alpha + jnp.sum(p, axis=-1, keepdims=True), acc * alpha + pv))
        return tuple(out)

    start = (jnp.full((bq, 1), NEG_INF, F32), jnp.zeros((bq, 1), F32), jnp.zeros((bq, MLA_VDIM), F32))
    carry = lax.fori_loop(0, i, lambda j, c: key_block(j, c, None), (start, start))
    carry = key_block(i, carry, causal)
    for hh in heads:
        _, l, acc = carry[hh]
        o_ref[:, v_lanes[hh]] = (acc / l).astype(o_ref.dtype)


def mla_prompt(q, k, v, *, batch, bq=512):
    m = q.shape[0]
    t = m // batch
    nq = t // bq
    return pl.pallas_call(
        functools.partial(_mla_prompt_kernel, bq=bq),
        grid=(batch, MLA_HEADS // 2, nq),
        in_specs=[pl.BlockSpec((bq, 2 * MLA_HEAD_PAD), lambda b, h, i: (b * nq + i, h)),
                  pl.BlockSpec((t, 2 * MLA_HEAD_PAD), lambda b, h, i: (b, h)),
                  pl.BlockSpec((t, 2 * MLA_VDIM), lambda b, h, i: (b, h))],
        out_specs=pl.BlockSpec((bq, 2 * MLA_VDIM), lambda b, h, i: (b * nq + i, h)),
        out_shape=jax.ShapeDtypeStruct((m, MLA_WIDTH), BF16),
        compiler_params=_params("parallel", "parallel", "arbitrary"),
        name="mla_prompt",
    )(q, k, v)


MLA_DECODE_Q_ROWS = 16


def _mla_decode_kernel(pt_ref, qn_ref, qr_ref, wkt_ref, ckv_hbm, kr_hbm, acc_ref, m_ref, l_ref,
                       lhs_ref, ckbuf, krbuf, sem, *, layer, n_pages, pages):
    s = pl.program_id(0)
    n_chunks = n_pages // pages
    last_chunk = pl.num_programs(0) * n_chunks - 1
    nope_w = MLA_HEADS * MLA_NOPE

    def copies(chunk, slot):
        out = []
        for p in range(pages):
            page = pt_ref[chunk * pages + p]
            out.append(pltpu.make_async_copy(ckv_hbm.at[layer, page], ckbuf.at[slot, p], sem.at[slot, 0]))
            out.append(pltpu.make_async_copy(kr_hbm.at[layer, page], krbuf.at[slot, p], sem.at[slot, 1]))
        return out

    def fetch(chunk, slot):
        for c in copies(chunk, slot):
            c.start()

    def arrive(chunk, slot):
        for c in copies(chunk, slot):
            c.wait()

    @pl.when(s == 0)
    def _():
        fetch(0, 0)

    wkt = wkt_ref[...]
    q_lat = _dot(qn_ref[...], wkt)
    pad = jnp.zeros((MLA_DECODE_Q_ROWS - MLA_HEADS, MLA_KV_LORA), F32)
    lhs_ref[0:nope_w, :] = wkt
    lhs_ref[nope_w:, :] = jnp.concatenate([q_lat, pad], axis=0).astype(BF16)
    qr = qr_ref[...]

    def attend(slot, carry):
        m, l, acc = carry
        lhs = lhs_ref[...]
        cks, scs = [], []
        for p in range(0, pages, 2):
            ck = jnp.concatenate([ckbuf[slot, p], ckbuf[slot, p + 1]], axis=0).astype(BF16)
            kr = jnp.concatenate([krbuf[slot, p], krbuf[slot, p + 1]], axis=1)
            r = _dot_nt(lhs, ck)
            s_raw = r[nope_w:nope_w + MLA_HEADS] + _dot(qr, kr.astype(BF16))
            ss_rot = jnp.sum(kr * kr, axis=0, keepdims=True)
            ss = []
            for h in range(MLA_HEADS):
                x = r[h * MLA_NOPE:(h + 1) * MLA_NOPE]
                part = (x * x).reshape(MLA_NOPE // 8, 8, x.shape[1]).sum(axis=0)
                ss.append(jnp.sum(part, axis=0, keepdims=True) + ss_rot)
            ss = jnp.concatenate(ss, axis=0)
            scs.append(s_raw * lax.rsqrt(ss * (1.0 / MLA_QK) + EPS))
            cks.append(ck)
        sc = jnp.concatenate(scs, axis=1)
        m_new = jnp.maximum(m, jnp.max(sc, axis=-1, keepdims=True))
        alpha = jnp.exp(m - m_new)
        pr = jnp.exp(sc - m_new)
        l = l * alpha + jnp.sum(pr, axis=-1, keepdims=True)
        acc = acc * alpha + _dot(pr.astype(BF16), jnp.concatenate(cks, axis=0))
        return m_new, l, acc

    def chunk_pair(cc, carry):
        first = s * n_chunks + 2 * cc
        fetch(first + 1, 1)
        arrive(first, 0)
        carry = attend(0, carry)
        fetch(jnp.minimum(first + 2, last_chunk), 0)
        arrive(first + 1, 1)
        return attend(1, carry)

    start = (jnp.full((MLA_HEADS, 1), NEG_INF, F32), jnp.zeros((MLA_HEADS, 1), F32),
             jnp.zeros((MLA_HEADS, MLA_KV_LORA), F32))
    m, l, acc = lax.fori_loop(0, n_chunks // 2, chunk_pair, start)

    @pl.when(s == pl.num_programs(0) - 1)
    def _():
        arrive(last_chunk, 0)

    acc_ref[...] = acc
    m_ref[...] = jnp.broadcast_to(m, m_ref.shape)
    l_ref[...] = jnp.broadcast_to(l, l_ref.shape)


def mla_decode(qn, qr, w_uk_t, cache_ckv, krope_t, page_table, layer, *, pages=16):
    b, n_pages = page_table.shape
    assert n_pages % (2 * pages) == 0, "chunks are processed in pairs"
    nope_w = MLA_HEADS * MLA_NOPE
    per_seq = lambda w: pl.BlockSpec((None, MLA_HEADS, w), lambda s, pt: (s, 0, 0))
    grid_spec = pltpu.PrefetchScalarGridSpec(
        num_scalar_prefetch=1,
        grid=(b,),
        in_specs=[per_seq(nope_w), per_seq(MLA_ROPE),
                  pl.BlockSpec((nope_w, MLA_KV_LORA), lambda s, pt: (0, 0)),
                  pl.BlockSpec(memory_space=pl.ANY), pl.BlockSpec(memory_space=pl.ANY)],
        out_specs=[per_seq(MLA_KV_LORA), per_seq(LANES), per_seq(LANES)],
        scratch_shapes=[pltpu.VMEM((nope_w + MLA_DECODE_Q_ROWS, MLA_KV_LORA), BF16),
                        pltpu.VMEM((2, pages, PAGE_SIZE, MLA_KV_LORA), F32),
                        pltpu.VMEM((2, pages, MLA_ROPE, PAGE_SIZE), F32),
                        pltpu.SemaphoreType.DMA((2, 2))],
    )
    return pl.pallas_call(
        functools.partial(_mla_decode_kernel, layer=layer, n_pages=n_pages, pages=pages),
        grid_spec=grid_spec,
        out_shape=[jax.ShapeDtypeStruct((b, MLA_HEADS, MLA_KV_LORA), F32),
                   jax.ShapeDtypeStruct((b, MLA_HEADS, LANES), F32),
                   jax.ShapeDtypeStruct((b, MLA_HEADS, LANES), F32)],
        compiler_params=_params("arbitrary"),
        name="mla_decode",
    )(page_table.reshape(-1), qn, qr, w_uk_t, cache_ckv, krope_t)


def _mla_decode_finish_kernel(acc_ref, m_ref, l_ref, q_ref, k_ref, v_ref, wv_ref, o_ref):
    for h in range(MLA_HEADS):
        hl = slice(h * MLA_HEAD_PAD, (h + 1) * MLA_HEAD_PAD)
        vl = slice(h * MLA_VDIM, (h + 1) * MLA_VDIM)
        s_self = jnp.sum(q_ref[:, hl] * k_ref[:, hl], axis=-1, keepdims=True)
        m_past = m_ref[:, h * LANES:h * LANES + 1]
        l_past = l_ref[:, h * LANES:h * LANES + 1]
        m_new = jnp.maximum(m_past, s_self)
        a_past = jnp.exp(m_past - m_new)
        a_self = jnp.exp(s_self - m_new)
        lat = acc_ref[:, h * MLA_KV_LORA:(h + 1) * MLA_KV_LORA].astype(BF16)
        val = _dot(lat, wv_ref[:, vl])
        o_ref[:, vl] = ((val * a_past + v_ref[:, vl] * a_self) / (l_past * a_past + a_self)).astype(o_ref.dtype)


def mla_decode_finish(acc, m, l, q, k_self, v_self, w_uv):
    b = q.shape[0]
    return pl.pallas_call(
        _mla_decode_finish_kernel,
        out_shape=jax.ShapeDtypeStruct((b, MLA_WIDTH), BF16),
        name="mla_decode_finish",
    )(acc.reshape(b, -1), m.reshape(b, -1), l.reshape(b, -1), q, k_self, v_self, w_uv)


def _merge_kernel(x_ref, gl_ref, oa_ref, ob_ref, oc_ref, wa_ref, wb_ref, wc_ref, wo_ref, y_ref):
    merged = None
    for n, (o_ref, w_ref) in enumerate(((oa_ref, wa_ref), (ob_ref, wb_ref), (oc_ref, wc_ref))):
        gate = jax.nn.sigmoid(gl_ref[:, n * D_MODEL:(n + 1) * D_MODEL])
        term = gate * _dot(o_ref[...], w_ref[...])
        merged = term if merged is None else merged + term
    y_ref[...] = x_ref[...] + _dot(merged.astype(BF16), wo_ref[...])


def merge(x, z, o_a, o_b, o_c, w_a, w_b, w_c, w_out, *, tm):
    m = x.shape[0]
    row = lambda w: pl.BlockSpec((tm, w), lambda i: (i, 0))
    wspec = lambda r: pl.BlockSpec((r, D_MODEL), lambda i: (0, 0))
    return pl.pallas_call(
        _merge_kernel,
        grid=(m // tm,),
        in_specs=[row(D_MODEL), pl.BlockSpec((tm, N_BRANCH * D_MODEL), lambda i: (i, 1)),
                  row(SB_WIDTH), row(GM_WIDTH), row(MLA_WIDTH),
                  wspec(SB_WIDTH), wspec(GM_WIDTH), wspec(MLA_WIDTH), wspec(D_MODEL)],
        out_specs=row(D_MODEL),
        out_shape=jax.ShapeDtypeStruct((m, D_MODEL), F32),
        compiler_params=_params("parallel"),
        name="merge",
    )(x, z, o_a, o_b, o_c, w_a, w_b, w_c, w_out)


def _swiglu_tile(h, wg, wu, wd):
    a = _dot(h, wg)
    return _dot((a * jax.nn.sigmoid(a) * _dot(h, wu)).astype(BF16), wd)


def _ffn_kernel(x_ref, g_ref, wg_ref, wu_ref, wd_ref, y_ref, h_ref, acc_ref):
    f = pl.program_id(1)

    @pl.when(f == 0)
    def _():
        x = x_ref[...]
        ms = jnp.mean(x * x, axis=-1, keepdims=True)
        h_ref[...] = (x * lax.rsqrt(ms + EPS) * g_ref[...]).astype(BF16)
        acc_ref[...] = x

    acc_ref[...] += _swiglu_tile(h_ref[...], wg_ref[...], wu_ref[...], wd_ref[...])

    @pl.when(f == pl.num_programs(1) - 1)
    def _():
        y_ref[...] = acc_ref[...]


def ffn(x, g, w_gate, w_up, w_down, *, tm, tf):
    m = x.shape[0]
    d_ff = w_gate.shape[1]
    return pl.pallas_call(
        _ffn_kernel,
        grid=(m // tm, d_ff // tf),
        in_specs=[pl.BlockSpec((tm, D_MODEL), lambda i, f: (i, 0)),
                  pl.BlockSpec((1, D_MODEL), lambda i, f: (0, 0)),
                  pl.BlockSpec((D_MODEL, tf), lambda i, f: (0, f)),
                  pl.BlockSpec((D_MODEL, tf), lambda i, f: (0, f)),
                  pl.BlockSpec((tf, D_MODEL), lambda i, f: (f, 0))],
        out_specs=pl.BlockSpec((tm, D_MODEL), lambda i, f: (i, 0)),
        out_shape=jax.ShapeDtypeStruct((m, D_MODEL), F32),
        scratch_shapes=[pltpu.VMEM((tm, D_MODEL), BF16), pltpu.VMEM((tm, D_MODEL), F32)],
        compiler_params=_params("parallel", "arbitrary"),
        name="ffn",
    )(x, g.reshape(1, -1), w_gate, w_up, w_down)


def _router_kernel(x_ref, g_ref, wr_ref, h_ref, idx_ref, wt_ref):
    x = x_ref[...]
    h = x * lax.rsqrt(jnp.mean(x * x, axis=-1, keepdims=True) + EPS) * g_ref[...]
    h_ref[...] = h
    h1 = h.astype(BF16)
    r = h - h1.astype(F32)
    h2 = r.astype(BF16)
    h3 = (r - h2.astype(F32)).astype(BF16)
    w1, w2, w3 = wr_ref[0], wr_ref[1], wr_ref[2]
    logits = (_dot(h1, w1) + (_dot(h1, w2) + _dot(h2, w1)) + (_dot(h1, w3) + _dot(h2, w2) + _dot(h3, w1)))
    lane = lax.broadcasted_iota(jnp.int32, logits.shape, 1)
    logits = jnp.where(lane < N_EXPERTS, logits, -jnp.inf)
    m1 = jnp.max(logits, axis=-1, keepdims=True)
    i1 = jnp.min(jnp.where(logits == m1, lane, LANES), axis=-1, keepdims=True)
    rest = jnp.where(lane == i1, -jnp.inf, logits)
    m2 = jnp.max(rest, axis=-1, keepdims=True)
    i2 = jnp.min(jnp.where(rest == m2, lane, LANES), axis=-1, keepdims=True)
    e = jnp.exp(m2 - m1)
    p1 = 1.0 / (1.0 + e)
    p2 = e / (1.0 + e)
    idx_ref[...] = jnp.where(lane == 0, i1, jnp.where(lane == 1, i2, 0))
    wt_ref[...] = jnp.where(lane == 0, p1, jnp.where(lane == 1, p2, 0.0))


def router(x, g, w_router, *, tm):
    m = x.shape[0]
    w = jnp.pad(w_router, ((0, 0), (0, LANES - N_EXPERTS)))
    w1 = w.astype(BF16)
    r = w - w1.astype(F32)
    w2 = r.astype(BF16)
    w3 = (r - w2.astype(F32)).astype(BF16)
    row = lambda wd: pl.BlockSpec((tm, wd), lambda i: (i, 0))
    return pl.pallas_call(
        _router_kernel,
        grid=(m // tm,),
        in_specs=[row(D_MODEL), pl.BlockSpec((1, D_MODEL), lambda i: (0, 0)),
                  pl.BlockSpec((3, D_MODEL, LANES), lambda i: (0, 0, 0))],
        out_specs=[row(D_MODEL), row(LANES), row(LANES)],
        out_shape=[jax.ShapeDtypeStruct((m, D_MODEL), F32), jax.ShapeDtypeStruct((m, LANES), jnp.int32),
                   jax.ShapeDtypeStruct((m, LANES), F32)],
        compiler_params=_params("parallel"),
        name="router",
    )(x, g.reshape(1, -1), jnp.stack([w1, w2, w3]))


def _experts_kernel(te_ref, nv_ref, src_ref, h_hbm, wg_ref, wu_ref, wd_ref, y_ref, hbuf, hb16, acc_ref, sem, *, tg):
    i, f = pl.program_id(0), pl.program_id(1)
    valid = i < nv_ref[0]

    def row_copy(r):
        return pltpu.make_async_copy(h_hbm.at[pl.ds(src_ref[0, 0, r], 1), :], hbuf.at[pl.ds(r, 1), :], sem)

    @pl.when(jnp.logical_and(valid, f == 0))
    def _():
        def start(r, c):
            row_copy(r).start()
            return c

        def wait(r, c):
            row_copy(r).wait()
            return c

        lax.fori_loop(0, tg, start, 0)
        lax.fori_loop(0, tg, wait, 0)
        hb16[...] = hbuf[...].astype(BF16)
        acc_ref[...] = jnp.zeros_like(acc_ref)

    @pl.when(valid)
    def _():
        acc_ref[...] += _swiglu_tile(hb16[...], wg_ref[...], wu_ref[...], wd_ref[...])

    @pl.when(f == pl.num_programs(1) - 1)
    def _():
        y_ref[...] = jnp.where(valid, acc_ref[...], 0.0)


def experts(h, src, tile_expert, n_valid, w_gate, w_up, w_down, *, tg, tf):
    n_tiles = src.shape[0]
    d_ff = w_gate.shape[2]
    nf = d_ff // tf

    def w_index(i, f, te, nv):
        ok = i < nv[0]
        return te[i], jnp.where(ok, f, nf - 1)

    grid_spec = pltpu.PrefetchScalarGridSpec(
        num_scalar_prefetch=2,
        grid=(n_tiles, nf),
        in_specs=[pl.BlockSpec((1, 1, tg), lambda i, f, te, nv: (i, 0, 0), memory_space=pltpu.SMEM),
                  pl.BlockSpec(memory_space=pl.ANY),
                  pl.BlockSpec((None, D_MODEL, tf), lambda i, f, te, nv: (w_index(i, f, te, nv)[0], 0, w_index(i, f, te, nv)[1])),
                  pl.BlockSpec((None, D_MODEL, tf), lambda i, f, te, nv: (w_index(i, f, te, nv)[0], 0, w_index(i, f, te, nv)[1])),
                  pl.BlockSpec((None, tf, D_MODEL), lambda i, f, te, nv: (w_index(i, f, te, nv)[0], w_index(i, f, te, nv)[1], 0))],
        out_specs=pl.BlockSpec((tg, D_MODEL), lambda i, f, te, nv: (i, 0)),
        scratch_shapes=[pltpu.VMEM((tg, D_MODEL), F32), pltpu.VMEM((tg, D_MODEL), BF16),
                        pltpu.VMEM((tg, D_MODEL), F32), pltpu.SemaphoreType.DMA(())],
    )
    return pl.pallas_call(
        functools.partial(_experts_kernel, tg=tg),
        grid_spec=grid_spec,
        out_shape=jax.ShapeDtypeStruct((n_tiles * tg, D_MODEL), F32),
        compiler_params=_params("arbitrary", "arbitrary"),
        name="experts",
    )(tile_expert, n_valid, src, h, w_gate, w_up, w_down)


def _combine_kernel(pos_ref, x_ref, wt_ref, y_hbm, o_ref, ybuf, sem, *, tc):
    def row_copy(r, k):
        return pltpu.make_async_copy(y_hbm.at[pl.ds(pos_ref[0, k, r], 1), :], ybuf.at[k, pl.ds(r, 1), :], sem)

    def start(r, c):
        row_copy(r, 0).start()
        row_copy(r, 1).start()
        return c

    def wait(r, c):
        row_copy(r, 0).wait()
        row_copy(r, 1).wait()
        return c

    lax.fori_loop(0, tc, start, 0)
    lax.fori_loop(0, tc, wait, 0)
    wt = wt_ref[...]
    o_ref[...] = x_ref[...] + (wt[:, 0:1] * ybuf[0] + wt[:, 1:2] * ybuf[1])


def combine(x, wt, pos, y, *, tc):
    m = x.shape[0]
    return pl.pallas_call(
        functools.partial(_combine_kernel, tc=tc),
        grid=(m // tc,),
        in_specs=[pl.BlockSpec((1, TOP_K, tc), lambda i: (i, 0, 0), memory_space=pltpu.SMEM),
                  pl.BlockSpec((tc, D_MODEL), lambda i: (i, 0)),
                  pl.BlockSpec((tc, LANES), lambda i: (i, 0)),
                  pl.BlockSpec(memory_space=pl.ANY)],
        out_specs=pl.BlockSpec((tc, D_MODEL), lambda i: (i, 0)),
        out_shape=jax.ShapeDtypeStruct((m, D_MODEL), F32),
        scratch_shapes=[pltpu.VMEM((TOP_K, tc, D_MODEL), F32), pltpu.SemaphoreType.DMA(())],
        compiler_params=_params("arbitrary"),
        name="combine",
    )(pos, x, wt, y)


def _route_plan(idx, *, tg):
    m = idx.shape[0]
    e_flat = idx.reshape(-1)
    onehot = (e_flat[:, None] == jnp.arange(N_EXPERTS)[None, :]).astype(jnp.int32)
    ranks = jnp.cumsum(onehot, axis=0) - onehot
    rank = jnp.sum(ranks * onehot, axis=1)
    counts = jnp.sum(onehot, axis=0)
    tiles = (counts + tg - 1) // tg
    tile_end = jnp.cumsum(tiles)
    start = (tile_end - tiles) * tg
    pos = start[e_flat] + rank
    n_tiles = (TOP_K * m) // tg + N_EXPERTS
    src = jnp.zeros((n_tiles * tg,), jnp.int32).at[pos].set(jnp.arange(TOP_K * m, dtype=jnp.int32) // TOP_K)
    tile_ids = jnp.arange(n_tiles)
    tile_expert = jnp.minimum(jnp.sum(tile_ids[:, None] >= tile_end[None, :], axis=1), N_EXPERTS - 1)
    n_valid = tile_end[-1:]
    last = tile_expert[jnp.maximum(n_valid[0] - 1, 0)]
    tile_expert = jnp.where(tile_ids < n_valid[0], tile_expert, last).astype(jnp.int32)
    return src.reshape(n_tiles, 1, tg), tile_expert, n_valid.astype(jnp.int32), pos.reshape(m, TOP_K)


def moe(x, g, w_router, w_gate, w_up, w_down, *, tm, tg, tf):
    m = x.shape[0]
    h, idx, wt = router(x, g, w_router, tm=tm)
    src, tile_expert, n_valid, pos = _route_plan(idx[:, :TOP_K], tg=tg)
    y = experts(h, src, tile_expert, n_valid, w_gate, w_up, w_down, tg=tg, tf=tf)
    pos_t = pos.reshape(m // tm, tm, TOP_K).transpose(0, 2, 1)
    return combine(x, wt, pos_t, y, tc=tm)


def _rope_tables(pos):
    t = pos.shape[0]
    inv_freq = ROPE_THETA ** (-jnp.arange(0, MLA_ROPE, 2, dtype=F32) / MLA_ROPE)
    ang = pos[:, None] * inv_freq[None, :]
    c, s = jnp.cos(ang), jnp.sin(ang)
    half = MLA_ROPE // 2
    zeros = lambda w: jnp.zeros((t, w), F32)
    tail = LANES - MLA_QK
    return (jnp.concatenate([jnp.ones((t, MLA_NOPE), F32), c, c, zeros(tail)], axis=1),
            jnp.concatenate([zeros(MLA_NOPE), -s, zeros(half), zeros(tail)], axis=1),
            jnp.concatenate([zeros(MLA_NOPE), zeros(half), s, zeros(tail)], axis=1))


def _pad_heads(w, used):
    rows, heads, _ = w.shape
    return jnp.pad(w, ((0, 0), (0, 0), (0, MLA_HEAD_PAD - used))).reshape(rows, heads * MLA_HEAD_PAD)


def _layer_weights(l, w):
    d = {}
    cols = jnp.split(w['w_in'][l], np.cumsum([SB_WIDTH, SB_KV_WIDTH, SB_KV_WIDTH, GM_WIDTH, GM_WIDTH, MLA_Q_LORA,
                                              MLA_KV_LORA, MLA_ROPE]).tolist(), axis=1)
    zc = lambda n: jnp.zeros((D_MODEL, n), F32)
    krope = jnp.concatenate([zc(MLA_NOPE), cols[7], zc(LANES - MLA_QK)], axis=1)
    used = Z_KROPE + LANES
    d['w_in'] = jnp.concatenate(cols[:7] + [krope, zc(Z_MIX - used), cols[8]], axis=1).astype(BF16)
    d['g_mix'] = w['g_mix'][l]
    d['g_sb_q'] = (jnp.tile(w['g_sb_q'][l], SB_HEADS) * HEAD_DIM ** -0.5).reshape(1, -1)
    d['g_sb_k'] = jnp.tile(w['g_sb_k'][l], SB_KV_HEADS).reshape(1, -1)
    d['g_gm_v'] = w['g_gm_v'][l].reshape(1, -1)
    d['g_mla_cq'] = w['g_mla_cq'][l].reshape(1, -1)
    d['g_mla_ckv'] = w['g_mla_ckv'][l].reshape(1, -1)
    d['w_uq'] = _pad_heads(w['w_mla_uq'][l], MLA_QK).astype(BF16)
    d['w_uk_pad'] = _pad_heads(w['w_mla_uk'][l], MLA_NOPE).astype(BF16)
    d['w_uk_t'] = w['w_mla_uk'][l].reshape(MLA_KV_LORA, -1).T.astype(BF16)
    d['w_uv'] = w['w_mla_uv'][l].reshape(MLA_KV_LORA, -1).astype(BF16)
    rot_gain = jnp.concatenate([jnp.ones((MLA_ROPE,), F32), jnp.zeros((LANES - MLA_QK,), F32)])
    d['gain_q'] = (jnp.tile(jnp.concatenate([w['g_mla_q'][l], rot_gain]), MLA_HEADS) * MLA_QK ** -0.5).reshape(1, -1)
    d['gain_k'] = jnp.tile(jnp.concatenate([w['g_mla_k'][l], rot_gain]), MLA_HEADS).reshape(1, -1)
    d['g_mla_k'] = w['g_mla_k'][l]
    d['w_gm_s'] = w['w_gm_s'][l]
    d['b_gm_s'] = w['b_gm_s'][l]
    d['b_gm_full'] = jnp.repeat(w['b_gm_s'][l].T, GM_GROUP_DIM, axis=1)
    for name in ('w_br_a', 'w_br_b', 'w_br_c', 'w_out'):
        d[name] = w[name][l].astype(BF16)
    d['g_ffn'] = w['g_ffn'][l]
    i = l // 2
    if l % 2 == 0:
        for name in ('w_ffn_gate', 'w_ffn_up', 'w_ffn_down'):
            d[name] = w[name][i].astype(BF16)
    else:
        d['w_router'] = w['w_router'][i]
        for name in ('w_exp_gate', 'w_exp_up', 'w_exp_down'):
            d[name] = w[name][i].astype(BF16)
    return d


def _mixer_inputs(x, d, tabs, *, tm, qk_dtype):
    z = norm_matmul(x, d['g_mix'], d['w_in'], tm=tm, tn=1024)
    q_a, k_a, v_a, u_b, v_b, cqn, ckv, krot = prep(z, d['g_sb_q'], d['g_sb_k'], d['g_gm_v'], d['g_mla_cq'],
                                                   d['g_mla_ckv'], tabs, tm=tm)
    q_c = mla_q(cqn, d['w_uq'], d['gain_q'], tabs, tm=tm, out_dtype=qk_dtype)
    k_c, v_c = mla_kv(ckv, krot, d['w_uk_pad'], d['w_uv'], d['gain_k'], tm=tm, out_dtype=qk_dtype)
    return z, q_a, k_a, v_a, u_b, v_b, q_c, k_c, v_c, ckv, krot


def _channel_mixer(x, l, d, *, tm, tg):
    if l % 2 == 0:
        return ffn(x, d['g_ffn'], d['w_ffn_gate'], d['w_ffn_up'], d['w_ffn_down'], tm=tm, tf=d['w_ffn_gate'].shape[1] // 2)
    return moe(x, d['g_ffn'], d['w_router'], d['w_exp_gate'], d['w_exp_up'], d['w_exp_down'], tm=tm, tg=tg,
               tf=d['w_exp_gate'].shape[2] // 4)


def _prompt_layer(x, l, d, tabs, batch):
    tm = 512
    z, q_a, k_a, v_a, u_b, v_b, q_c, k_c, v_c, ckv, krot = _mixer_inputs(x, d, tabs, tm=tm, qk_dtype=BF16)
    o_a = sb_prompt(q_a, k_a, v_a, batch=batch)
    o_b = gmlp(u_b, v_b, d['w_gm_s'], d['b_gm_full'])
    o_c = mla_prompt(q_c, k_c, v_c, batch=batch)
    x = merge(x, z, o_a, o_b, o_c, d['w_br_a'], d['w_br_b'], d['w_br_c'], d['w_out'], tm=tm)
    x = _channel_mixer(x, l, d, tm=tm, tg=512)
    return x, (k_a, v_a, ckv, krot[:, MLA_NOPE:MLA_QK])


def _sample_layer(x, l, d, tabs, caches, page_table):
    tm = x.shape[0]
    cache_sb_k, cache_sb_v, cache_ckv, cache_krope = caches
    z, q_a, k_a, v_a, u_b, v_b, q_c, k_c, v_c, ckv, krot = _mixer_inputs(x, d, tabs, tm=tm, qk_dtype=F32)
    kv_of_head = np.arange(SB_HEADS) // SB_GROUP
    lane_kv = np.arange(SB_KV_WIDTH) // HEAD_DIM
    head_mask = jnp.asarray(kv_of_head[:, None] == lane_kv[None, :], q_a.dtype)
    q_bd = jnp.tile(q_a.reshape(tm, SB_HEADS, 1, HEAD_DIM), (1, 1, SB_KV_HEADS, 1)).reshape(tm, SB_HEADS, SB_KV_WIDTH)
    o_full = sb_decode(q_bd * head_mask, cache_sb_k, cache_sb_v, page_table, l)
    o_full = o_full.reshape(tm, SB_KV_HEADS, SB_GROUP, SB_KV_HEADS, HEAD_DIM)
    o_a = jnp.stack([o_full[:, h, :, h, :] for h in range(SB_KV_HEADS)], axis=1).reshape(tm, SB_WIDTH).astype(BF16)
    o_b = gmlp_first(u_b, v_b, d['w_gm_s'], d['b_gm_s'])
    q_heads = q_c.reshape(tm, MLA_HEADS, MLA_HEAD_PAD)
    eye = jnp.eye(MLA_HEADS, dtype=F32)
    qn = (q_heads[:, :, :MLA_NOPE] * d['g_mla_k'])[:, :, None, :] * eye[None, :, :, None]
    qn = qn.reshape(tm, MLA_HEADS, MLA_HEADS * MLA_NOPE).astype(BF16)
    qr = q_heads[:, :, MLA_NOPE:MLA_QK].astype(BF16)
    acc, m_run, l_run = mla_decode(qn, qr, d['w_uk_t'], cache_ckv, cache_krope, page_table, l)
    o_c = mla_decode_finish(acc, m_run, l_run, q_c, k_c, v_c, d['w_uv'])
    x = merge(x, z, o_a, o_b, o_c, d['w_br_a'], d['w_br_b'], d['w_br_c'], d['w_out'], tm=tm)
    x = _channel_mixer(x, l, d, tm=tm, tg=128)
    return x, (k_a, v_a, ckv, krot[:, MLA_NOPE:MLA_QK], v_b)


def kernel(x_prompt, x_sample, cache_sb_k, cache_sb_v, cache_mla_ckv, cache_mla_krope, page_table, g_mix, w_in, g_sb_q, g_sb_k, g_gm_v, w_gm_s, b_gm_s, g_mla_cq, g_mla_ckv, w_mla_uq, w_mla_uk, w_mla_uv, g_mla_q, g_mla_k, w_br_a, w_br_b, w_br_c, w_out, g_ffn, w_ffn_gate, w_ffn_up, w_ffn_down, w_router, w_exp_gate, w_exp_up, w_exp_down):
    weights = dict(g_mix=g_mix, w_in=w_in, g_sb_q=g_sb_q, g_sb_k=g_sb_k, g_gm_v=g_gm_v, w_gm_s=w_gm_s, b_gm_s=b_gm_s,
                   g_mla_cq=g_mla_cq, g_mla_ckv=g_mla_ckv, w_mla_uq=w_mla_uq, w_mla_uk=w_mla_uk, w_mla_uv=w_mla_uv,
                   g_mla_q=g_mla_q, g_mla_k=g_mla_k, w_br_a=w_br_a, w_br_b=w_br_b, w_br_c=w_br_c, w_out=w_out,
                   g_ffn=g_ffn, w_ffn_gate=w_ffn_gate, w_ffn_up=w_ffn_up, w_ffn_down=w_ffn_down, w_router=w_router,
                   w_exp_gate=w_exp_gate, w_exp_up=w_exp_up, w_exp_down=w_exp_down)
    batch, seq, _ = x_prompt.shape
    dec_batch, dec_seq, _ = x_sample.shape
    depth = g_mix.shape[0]
    past_len = page_table.shape[1] * PAGE_SIZE
    assert dec_seq == 1, "the sample path handles one new token per sequence"
    tabs_p = _rope_tables(jnp.arange(seq, dtype=F32))
    tabs_s = tuple(jnp.broadcast_to(t, (dec_batch, LANES))
                   for t in _rope_tables(jnp.arange(dec_seq, dtype=F32) + past_len))
    caches = (sb_feature_major(cache_sb_k), sb_feature_major(cache_sb_v), cache_mla_ckv,
              jnp.transpose(cache_mla_krope, (0, 1, 3, 2)))
    h_p = x_prompt.reshape(batch * seq, D_MODEL)
    h_s = x_sample.reshape(dec_batch * dec_seq, D_MODEL)
    outs_p, outs_s = [], []
    for l in range(depth):
        d = _layer_weights(l, weights)
        h_p, new_p = _prompt_layer(h_p, l, d, tabs_p, batch)
        h_s, new_s = _sample_layer(h_s, l, d, tabs_s, caches, page_table)
        outs_p.append(new_p)
        outs_s.append(new_s)

    def stack(outs, n, shape):
        return jnp.stack([o[n].reshape(shape) for o in outs])

    kv_p = (batch, seq, SB_KV_HEADS, HEAD_DIM)
    kv_s = (dec_batch, dec_seq, SB_KV_HEADS, HEAD_DIM)
    return (h_p.reshape(batch, seq, D_MODEL), h_s.reshape(dec_batch, dec_seq, D_MODEL),
            stack(outs_p, 0, kv_p), stack(outs_p, 1, kv_p),
            stack(outs_p, 2, (batch, seq, MLA_KV_LORA)), stack(outs_p, 3, (batch, seq, MLA_ROPE)),
            stack(outs_s, 0, kv_s), stack(outs_s, 1, kv_s),
            stack(outs_s, 2, (dec_batch, dec_seq, MLA_KV_LORA)), stack(outs_s, 3, (dec_batch, dec_seq, MLA_ROPE)),
            stack(outs_s, 4, (dec_batch, dec_seq, GM_WIDTH)))
```

```python
import functools

import numpy as np
import jax
import jax.numpy as jnp
from jax import lax
from jax.experimental import pallas as pl
from jax.experimental.pallas import tpu as pltpu

F32 = jnp.float32
BF16 = jnp.bfloat16
EPS = 1e-6
NEG_INF = -1e30
LANES = 128

D_MODEL = 1024
HEAD_DIM = 64
SB_HEADS = 8
SB_KV_HEADS = 2
SB_GROUP = SB_HEADS // SB_KV_HEADS
SB_WIDTH = SB_HEADS * HEAD_DIM
SB_KV_WIDTH = SB_KV_HEADS * HEAD_DIM
GM_GROUPS = 8
GM_GROUP_DIM = 64
GM_CHUNK = 128
GM_WIDTH = GM_GROUPS * GM_GROUP_DIM
MLA_HEADS = 8
MLA_NOPE = 64
MLA_ROPE = 32
MLA_QK = MLA_NOPE + MLA_ROPE
MLA_VDIM = 64
MLA_Q_LORA = 768
MLA_KV_LORA = 256
MLA_WIDTH = MLA_HEADS * MLA_VDIM
MLA_HEAD_PAD = LANES
ROPE_THETA = 10000.0
N_BRANCH = 3
N_EXPERTS = 8
TOP_K = 2
PAGE_SIZE = 128

Z_SB_Q = 0
Z_SB_K = Z_SB_Q + SB_WIDTH
Z_SB_V = Z_SB_K + SB_KV_WIDTH
Z_GM_U = Z_SB_V + SB_KV_WIDTH
Z_GM_V = Z_GM_U + GM_WIDTH
Z_CQ = Z_GM_V + GM_WIDTH
Z_CKV = Z_CQ + MLA_Q_LORA
Z_KROPE = Z_CKV + MLA_KV_LORA
Z_MIX = 3 * D_MODEL
Z_COLS = Z_MIX + N_BRANCH * D_MODEL

VMEM_LIMIT = 56 * 1024 * 1024


def _params(*sem, vmem=VMEM_LIMIT):
    return pltpu.CompilerParams(dimension_semantics=sem, vmem_limit_bytes=vmem)


def _dot(a, b):
    return jnp.dot(a, b, preferred_element_type=F32)


def _dot_nt(a, b):
    return lax.dot_general(a, b, (((1,), (1,)), ((), ())), preferred_element_type=F32)


def _split_bf16(x):
    hi = x.astype(BF16)
    lo = (x - hi.astype(F32)).astype(BF16)
    return hi, lo


def _norm_matmul_kernel(x_ref, g_ref, w_ref, o_ref, h_ref):
    @pl.when(pl.program_id(1) == 0)
    def _():
        x = x_ref[...]
        ms = jnp.mean(x * x, axis=-1, keepdims=True)
        h_ref[...] = (x * lax.rsqrt(ms + EPS) * g_ref[...]).astype(BF16)

    o_ref[...] = _dot(h_ref[...], w_ref[...])


def norm_matmul(x, g, w, *, tm, tn):
    m, k = x.shape
    n = w.shape[1]
    return pl.pallas_call(
        _norm_matmul_kernel,
        grid=(m // tm, n // tn),
        in_specs=[pl.BlockSpec((tm, k), lambda i, j: (i, 0)),
                  pl.BlockSpec((1, k), lambda i, j: (0, 0)),
                  pl.BlockSpec((k, tn), lambda i, j: (0, j))],
        out_specs=pl.BlockSpec((tm, tn), lambda i, j: (i, j)),
        out_shape=jax.ShapeDtypeStruct((m, n), F32),
        scratch_shapes=[pltpu.VMEM((tm, k), BF16)],
        compiler_params=_params("parallel", "arbitrary"),
        name="norm_matmul",
    )(x, g.reshape(1, k), w)


def _gelu(x):
    c = np.float32(np.sqrt(2.0 / np.pi))
    return x * (0.5 * (1.0 + jnp.tanh(c * (x + 0.044715 * (x * x * x)))))


def _rope(x, c, s1, s2, width):
    return x * c + pltpu.roll(x, width - MLA_ROPE // 2, 1) * s1 + pltpu.roll(x, MLA_ROPE // 2, 1) * s2


def _group_sumsq(x, bd):
    hi, lo = _split_bf16(x * x)
    return _dot(hi, bd) + _dot(lo, bd)


def _prep_kernel(z_ref, gq_ref, gk_ref, gv_ref, gcq_ref, gckv_ref, c_ref, s1_ref, s2_ref, bdq_ref, bdk_ref,
                 qa_ref, ka_ref, va_ref, ub_ref, vb_ref, cqn_ref, ckv_ref, krot_ref):
    inv_d = 1.0 / HEAD_DIM
    q = z_ref[:, Z_SB_Q:Z_SB_Q + SB_WIDTH]
    qa_ref[...] = (q * lax.rsqrt(_group_sumsq(q, bdq_ref[...]) * inv_d + EPS) * gq_ref[...]).astype(qa_ref.dtype)
    k = z_ref[:, Z_SB_K:Z_SB_K + SB_KV_WIDTH]
    ka_ref[...] = k * lax.rsqrt(_group_sumsq(k, bdk_ref[...]) * inv_d + EPS) * gk_ref[...]
    va_ref[...] = z_ref[:, Z_SB_V:Z_SB_V + SB_KV_WIDTH]
    ub_ref[...] = _gelu(z_ref[:, Z_GM_U:Z_GM_U + GM_WIDTH]).astype(ub_ref.dtype)
    gv = _gelu(z_ref[:, Z_GM_V:Z_GM_V + GM_WIDTH])
    vb_ref[...] = gv * lax.rsqrt(jnp.mean(gv * gv, axis=-1, keepdims=True) + EPS) * gv_ref[...]
    cq = z_ref[:, Z_CQ:Z_CQ + MLA_Q_LORA]
    cqn_ref[...] = (cq * lax.rsqrt(jnp.mean(cq * cq, axis=-1, keepdims=True) + EPS) * gcq_ref[...]).astype(cqn_ref.dtype)
    ckv = z_ref[:, Z_CKV:Z_CKV + MLA_KV_LORA]
    ckv_ref[...] = ckv * lax.rsqrt(jnp.mean(ckv * ckv, axis=-1, keepdims=True) + EPS) * gckv_ref[...]
    kr = z_ref[:, Z_KROPE:Z_KROPE + LANES]
    krot_ref[...] = _rope(kr, c_ref[...], s1_ref[...], s2_ref[...], LANES)


def _block_diag_ones(width, group):
    idx = np.arange(width) // group
    return jnp.asarray(idx[:, None] == idx[None, :], BF16)


def prep(z, gq, gk, gv, gcq, gckv, tabs, *, tm):
    m = z.shape[0]
    nt = tabs[0].shape[0] // tm
    row = lambda w: pl.BlockSpec((tm, w), lambda i: (i, 0))
    vec = lambda w: pl.BlockSpec((1, w), lambda i: (0, 0))
    tab = pl.BlockSpec((tm, LANES), lambda i: (i % nt, 0))
    full = lambda w: pl.BlockSpec((w, w), lambda i: (0, 0))
    outs = [(SB_WIDTH, BF16), (SB_KV_WIDTH, F32), (SB_KV_WIDTH, F32), (GM_WIDTH, BF16), (GM_WIDTH, F32),
            (MLA_Q_LORA, BF16), (MLA_KV_LORA, F32), (LANES, F32)]
    return pl.pallas_call(
        _prep_kernel,
        grid=(m // tm,),
        in_specs=[pl.BlockSpec((tm, Z_MIX), lambda i: (i, 0)),
                  vec(SB_WIDTH), vec(SB_KV_WIDTH), vec(GM_WIDTH), vec(MLA_Q_LORA), vec(MLA_KV_LORA),
                  tab, tab, tab, full(SB_WIDTH), full(SB_KV_WIDTH)],
        out_specs=[row(w) for w, _ in outs],
        out_shape=[jax.ShapeDtypeStruct((m, w), dt) for w, dt in outs],
        compiler_params=_params("parallel"),
        name="prep",
    )(z, gq, gk, gv, gcq, gckv, *tabs, _block_diag_ones(SB_WIDTH, HEAD_DIM), _block_diag_ones(SB_KV_WIDTH, HEAD_DIM))


def _head_norm(xh, gain):
    ss = jnp.sum(xh * xh, axis=-1, keepdims=True) * (1.0 / MLA_QK)
    return xh * lax.rsqrt(ss + EPS) * gain


def _mla_q_kernel(cq_ref, w_ref, gain_ref, c_ref, s1_ref, s2_ref, o_ref):
    q = _dot(cq_ref[...], w_ref[...])
    c, s1, s2 = c_ref[...], s1_ref[...], s2_ref[...]
    for h in range(MLA_HEADS):
        sl = slice(h * MLA_HEAD_PAD, (h + 1) * MLA_HEAD_PAD)
        xh = _rope(q[:, sl], c, s1, s2, LANES)
        o_ref[:, sl] = _head_norm(xh, gain_ref[:, sl]).astype(o_ref.dtype)


def mla_q(cqn, w_uq_pad, gain, tabs, *, tm, out_dtype):
    m = cqn.shape[0]
    nt = tabs[0].shape[0] // tm
    width = MLA_HEADS * MLA_HEAD_PAD
    tab = pl.BlockSpec((tm, LANES), lambda i: (i % nt, 0))
    return pl.pallas_call(
        _mla_q_kernel,
        grid=(m // tm,),
        in_specs=[pl.BlockSpec((tm, MLA_Q_LORA), lambda i: (i, 0)),
                  pl.BlockSpec((MLA_Q_LORA, width), lambda i: (0, 0)),
                  pl.BlockSpec((1, width), lambda i: (0, 0)), tab, tab, tab],
        out_specs=pl.BlockSpec((tm, width), lambda i: (i, 0)),
        out_shape=jax.ShapeDtypeStruct((m, width), out_dtype),
        compiler_params=_params("parallel"),
        name="mla_q",
    )(cqn, w_uq_pad, gain, *tabs)


def _mla_kv_kernel(ckv_ref, krot_ref, wk_ref, wv_ref, gain_ref, k_ref, v_ref):
    ckv = ckv_ref[...].astype(BF16)
    kn = _dot(ckv, wk_ref[...])
    krot = krot_ref[...]
    for h in range(MLA_HEADS):
        sl = slice(h * MLA_HEAD_PAD, (h + 1) * MLA_HEAD_PAD)
        k_ref[:, sl] = _head_norm(kn[:, sl] + krot, gain_ref[:, sl]).astype(k_ref.dtype)
    v_ref[...] = _dot(ckv, wv_ref[...]).astype(v_ref.dtype)


def mla_kv(ckv, krot, w_uk_pad, w_uv, gain, *, tm, out_dtype):
    m = ckv.shape[0]
    width = MLA_HEADS * MLA_HEAD_PAD
    return pl.pallas_call(
        _mla_kv_kernel,
        grid=(m // tm,),
        in_specs=[pl.BlockSpec((tm, MLA_KV_LORA), lambda i: (i, 0)),
                  pl.BlockSpec((tm, LANES), lambda i: (i, 0)),
                  pl.BlockSpec((MLA_KV_LORA, width), lambda i: (0, 0)),
                  pl.BlockSpec((MLA_KV_LORA, MLA_WIDTH), lambda i: (0, 0)),
                  pl.BlockSpec((1, width), lambda i: (0, 0))],
        out_specs=[pl.BlockSpec((tm, width), lambda i: (i, 0)), pl.BlockSpec((tm, MLA_WIDTH), lambda i: (i, 0))],
        out_shape=[jax.ShapeDtypeStruct((m, width), out_dtype), jax.ShapeDtypeStruct((m, MLA_WIDTH), out_dtype)],
        compiler_params=_params("parallel"),
        name="mla_kv",
    )(ckv, krot, w_uk_pad, w_uv, gain)


KEEP_FLOOR = -104.0


def _sb_block(z, mask, keep, u):
    l1p = jnp.log1p(jnp.exp(-jnp.abs(z)))
    log_beta = jnp.minimum(z, 0.0) - l1p
    log_keep = jnp.minimum(-z, 0.0) - l1p
    if mask is not None:
        log_keep = jnp.where(mask, log_keep, 0.0)
    hi, lo = _split_bf16(log_keep)
    after = _dot(hi, u) + _dot(lo, u)
    w = jnp.exp(log_beta + after + keep)
    if mask is not None:
        w = jnp.where(mask, w, 0.0)
    return w, keep + (after[:, 0:1] + log_keep[:, 0:1])


def _later_keys_ones(n):
    return jnp.asarray(np.arange(n)[:, None] > np.arange(n)[None, :], BF16)


def _sb_prompt_kernel(q_ref, k_ref, v_ref, u_ref, o_ref, *, bq):
    i = pl.program_id(1)
    u = u_ref[...]
    rows = SB_GROUP * bq
    r = lax.broadcasted_iota(jnp.int32, (rows, bq), 0) & (bq - 1)
    s = lax.broadcasted_iota(jnp.int32, (rows, bq), 1)
    diag_mask = s < r
    kv_heads = range(SB_KV_HEADS)
    lanes = [slice(h * HEAD_DIM, (h + 1) * HEAD_DIM) for h in kv_heads]
    qs = [jnp.concatenate([q_ref[:, (h * SB_GROUP + g) * HEAD_DIM:(h * SB_GROUP + g + 1) * HEAD_DIM]
                           for g in range(SB_GROUP)], axis=0) for h in kv_heads]

    def block(j, mask, accs, keeps):
        new_accs, new_keeps = [], []
        for h in kv_heads:
            kb = k_ref[pl.ds(j * bq, bq), lanes[h]].astype(BF16)
            vb = v_ref[pl.ds(j * bq, bq), lanes[h]].astype(BF16)
            w, keep = _sb_block(_dot_nt(qs[h], kb), mask, keeps[h], u)
            pv = _dot(w.astype(BF16), vb)
            new_accs.append(pv if accs is None else accs[h] + pv)
            new_keeps.append(keep)
        return tuple(new_accs), tuple(new_keeps)

    accs, keeps = block(i, diag_mask, None, (jnp.zeros((rows, 1), F32),) * SB_KV_HEADS)

    def more(carry):
        jj, _, keeps = carry
        return jnp.logical_and(jj <= i, jnp.max(functools.reduce(jnp.maximum, keeps)) > KEEP_FLOOR)

    def body(carry):
        jj, accs, keeps = carry
        accs, keeps = block(i - jj, None, accs, keeps)
        return jj + 1, accs, keeps

    _, accs, _ = lax.while_loop(more, body, (jnp.int32(1), accs, keeps))
    for h in kv_heads:
        for g in range(SB_GROUP):
            col = (h * SB_GROUP + g) * HEAD_DIM
            o_ref[:, col:col + HEAD_DIM] = accs[h][g * bq:(g + 1) * bq].astype(o_ref.dtype)


def sb_prompt(q, k, v, *, batch, bq=128):
    m = q.shape[0]
    t = m // batch
    nq = t // bq
    return pl.pallas_call(
        functools.partial(_sb_prompt_kernel, bq=bq),
        grid=(batch, nq),
        in_specs=[pl.BlockSpec((bq, SB_WIDTH), lambda b, i: (b * nq + i, 0)),
                  pl.BlockSpec((t, SB_KV_WIDTH), lambda b, i: (b, 0)),
                  pl.BlockSpec((t, SB_KV_WIDTH), lambda b, i: (b, 0)),
                  pl.BlockSpec((bq, bq), lambda b, i: (0, 0))],
        out_specs=pl.BlockSpec((bq, SB_WIDTH), lambda b, i: (b * nq + i, 0)),
        out_shape=jax.ShapeDtypeStruct((m, SB_WIDTH), BF16),
        compiler_params=_params("parallel", "arbitrary"),
        name="sb_prompt",
    )(q, k, v, _later_keys_ones(bq))


def _sb_decode_kernel(pt_ref, q_ref, u_ref, k_hbm, v_hbm, o_ref, kbuf, vbuf, sem, *, layer, n_pages, pages):
    s = pl.program_id(0)
    slot = s % 2

    def copies(seq, chunk, slot):
        out = []
        for p in range(pages):
            page = pt_ref[seq * n_pages + n_pages - 1 - (chunk * pages + p)]
            out.append(pltpu.make_async_copy(k_hbm.at[layer, page], kbuf.at[slot, p], sem.at[slot, 0]))
            out.append(pltpu.make_async_copy(v_hbm.at[layer, page], vbuf.at[slot, p], sem.at[slot, 1]))
        return out

    def fetch(seq, chunk, slot):
        for c in copies(seq, chunk, slot):
            c.start()

    def arrive(seq, chunk, slot):
        for c in copies(seq, chunk, slot):
            c.wait()

    @pl.when(s == 0)
    def _():
        fetch(s, 0, slot)

    @pl.when(s + 1 < pl.num_programs(0))
    def _():
        fetch(s + 1, 0, 1 - slot)

    q = q_ref[...]
    u = u_ref[...]

    def attend(acc, keep):
        for p in range(pages):
            w, keep = _sb_block(_dot(q, kbuf[slot, p].astype(BF16)), None, keep, u)
            acc = acc + _dot_nt(w.astype(BF16), vbuf[slot, p].astype(BF16))
        return acc, keep

    arrive(s, 0, slot)
    acc, keep = attend(jnp.zeros(o_ref.shape, F32), jnp.zeros((SB_HEADS, 1), F32))

    def more(carry):
        chunk, _, keep = carry
        return jnp.logical_and(chunk < n_pages // pages, jnp.max(keep) > KEEP_FLOOR)

    def body(carry):
        chunk, acc, keep = carry
        fetch(s, chunk, slot)
        arrive(s, chunk, slot)
        acc, keep = attend(acc, keep)
        return chunk + 1, acc, keep

    _, acc, _ = lax.while_loop(more, body, (jnp.int32(1), acc, keep))
    o_ref[...] = acc


def sb_feature_major(cache):
    return jnp.transpose(cache, (0, 1, 3, 4, 2)).reshape(cache.shape[0], cache.shape[1], SB_KV_WIDTH, PAGE_SIZE)


def sb_decode(q_bd, cache_k_t, cache_v_t, page_table, layer, *, pages=2):
    b, n_pages = page_table.shape
    grid_spec = pltpu.PrefetchScalarGridSpec(
        num_scalar_prefetch=1,
        grid=(b,),
        in_specs=[pl.BlockSpec((None, SB_HEADS, SB_KV_WIDTH), lambda s, pt: (s, 0, 0)),
                  pl.BlockSpec((PAGE_SIZE, PAGE_SIZE), lambda s, pt: (0, 0)),
                  pl.BlockSpec(memory_space=pl.ANY), pl.BlockSpec(memory_space=pl.ANY)],
        out_specs=pl.BlockSpec((None, SB_HEADS, SB_KV_WIDTH), lambda s, pt: (s, 0, 0)),
        scratch_shapes=[pltpu.VMEM((2, pages, SB_KV_WIDTH, PAGE_SIZE), F32),
                        pltpu.VMEM((2, pages, SB_KV_WIDTH, PAGE_SIZE), F32),
                        pltpu.SemaphoreType.DMA((2, 2))],
    )
    return pl.pallas_call(
        functools.partial(_sb_decode_kernel, layer=layer, n_pages=n_pages, pages=pages),
        grid_spec=grid_spec,
        out_shape=jax.ShapeDtypeStruct((b, SB_HEADS, SB_KV_WIDTH), F32),
        compiler_params=_params("arbitrary"),
        name="sb_decode",
    )(page_table.reshape(-1), q_bd, _later_keys_ones(PAGE_SIZE), cache_k_t, cache_v_t)


def _gmlp_kernel(u_ref, v_ref, w_ref, b_ref, o_ref, *, chunks):
    t = lax.broadcasted_iota(jnp.int32, (GM_CHUNK, GM_CHUNK), 0)
    s = lax.broadcasted_iota(jnp.int32, (GM_CHUNK, GM_CHUNK), 1)
    causal = s <= t
    group = lax.broadcasted_iota(jnp.int32, (GM_CHUNK, GM_WIDTH), 1) // GM_GROUP_DIM
    ws = [jnp.where(causal, w_ref[g], 0.0).astype(BF16) for g in range(GM_GROUPS)]
    for c in range(chunks):
        rows = slice(c * GM_CHUNK, (c + 1) * GM_CHUNK)
        vc = v_ref[rows, :].astype(BF16)
        mixed = b_ref[...]
        for g in range(GM_GROUPS):
            mixed = mixed + jnp.where(group == g, _dot(ws[g], vc), 0.0)
        o_ref[rows, :] = (u_ref[rows, :].astype(F32) * mixed).astype(o_ref.dtype)


def gmlp(u, v, w_s, b_full, *, chunks=4):
    m = u.shape[0]
    tm = chunks * GM_CHUNK
    return pl.pallas_call(
        functools.partial(_gmlp_kernel, chunks=chunks),
        grid=(m // tm,),
        in_specs=[pl.BlockSpec((tm, GM_WIDTH), lambda i: (i, 0)),
                  pl.BlockSpec((tm, GM_WIDTH), lambda i: (i, 0)),
                  pl.BlockSpec((GM_GROUPS, GM_CHUNK, GM_CHUNK), lambda i: (0, 0, 0)),
                  pl.BlockSpec((GM_CHUNK, GM_WIDTH), lambda i: (0, 0))],
        out_specs=pl.BlockSpec((tm, GM_WIDTH), lambda i: (i, 0)),
        out_shape=jax.ShapeDtypeStruct((m, GM_WIDTH), BF16),
        compiler_params=_params("parallel"),
        name="gmlp",
    )(u, v, w_s, b_full)


def _gmlp_first_kernel(u_ref, v_ref, w_ref, b_ref, o_ref):
    o_ref[...] = (u_ref[...].astype(F32) * (w_ref[...] * v_ref[...] + b_ref[...])).astype(o_ref.dtype)


def gmlp_first(u, v, w_s, b_s):
    m = u.shape[0]
    w0 = jnp.repeat(w_s[:, 0, 0], GM_GROUP_DIM).reshape(1, GM_WIDTH)
    b0 = jnp.repeat(b_s[:, 0], GM_GROUP_DIM).reshape(1, GM_WIDTH)
    return pl.pallas_call(
        _gmlp_first_kernel,
        out_shape=jax.ShapeDtypeStruct((m, GM_WIDTH), BF16),
        name="gmlp_first",
    )(u, v, w0, b0)


def _mla_prompt_kernel(q_ref, k_ref, v_ref, o_ref, *, bq):
    i = pl.program_id(2)
    t = lax.broadcasted_iota(jnp.int32, (bq, bq), 0)
    s = lax.broadcasted_iota(jnp.int32, (bq, bq), 1)
    causal = s <= t
    heads = range(2)
    qk_lanes = [slice(hh * MLA_HEAD_PAD, (hh + 1) * MLA_HEAD_PAD) for hh in heads]
    v_lanes = [slice(hh * MLA_VDIM, (hh + 1) * MLA_VDIM) for hh in heads]

    def key_block(j, carry, mask):
        out = []
        for hh in heads:
            m, l, acc = carry[hh]
            sc = _dot_nt(q_ref[:, qk_lanes[hh]], k_ref[pl.ds(j * bq, bq), qk_lanes[hh]])
            if mask is not None:
                sc = jnp.where(mask, sc, NEG_INF)
            m_new = jnp.maximum(m, jnp.max(sc, axis=-1, keepdims=True))
            alpha = jnp.exp(m - m_new)
            p = jnp.exp(sc - m_new)
            pv = _dot(p.astype(BF16), v_ref[pl.ds(j * bq, bq), v_lanes[hh]])
            out.append((m_new, l * alpha + jnp.sum(p, axis=-1, keepdims=True), acc * alpha + pv))
        return tuple(out)

    start = (jnp.full((bq, 1), NEG_INF, F32), jnp.zeros((bq, 1), F32), jnp.zeros((bq, MLA_VDIM), F32))
    carry = lax.fori_loop(0, i, lambda j, c: key_block(j, c, None), (start, start))
    carry = key_block(i, carry, causal)
    for hh in heads:
        _, l, acc = carry[hh]
        o_ref[:, v_lanes[hh]] = (acc / l).astype(o_ref.dtype)


def mla_prompt(q, k, v, *, batch, bq=512):
    m = q.shape[0]
    t = m // batch
    nq = t // bq
    return pl.pallas_call(
        functools.partial(_mla_prompt_kernel, bq=bq),
        grid=(batch, MLA_HEADS // 2, nq),
        in_specs=[pl.BlockSpec((bq, 2 * MLA_HEAD_PAD), lambda b, h, i: (b * nq + i, h)),
                  pl.BlockSpec((t, 2 * MLA_HEAD_PAD), lambda b, h, i: (b, h)),
                  pl.BlockSpec((t, 2 * MLA_VDIM), lambda b, h, i: (b, h))],
        out_specs=pl.BlockSpec((bq, 2 * MLA_VDIM), lambda b, h, i: (b * nq + i, h)),
        out_shape=jax.ShapeDtypeStruct((m, MLA_WIDTH), BF16),
        compiler_params=_params("parallel", "parallel", "arbitrary"),
        name="mla_prompt",
    )(q, k, v)


MLA_DECODE_Q_ROWS = 16


def _mla_decode_kernel(pt_ref, qn_ref, qr_ref, wkt_ref, ckv_hbm, kr_hbm, acc_ref, m_ref, l_ref,
                       lhs_ref, ckbuf, krbuf, sem, *, layer, n_pages, pages):
    s = pl.program_id(0)
    n_chunks = n_pages // pages
    last_chunk = pl.num_programs(0) * n_chunks - 1
    nope_w = MLA_HEADS * MLA_NOPE

    def copies(chunk, slot):
        out = []
        for p in range(pages):
            page = pt_ref[chunk * pages + p]
            out.append(pltpu.make_async_copy(ckv_hbm.at[layer, page], ckbuf.at[slot, p], sem.at[slot, 0]))
            out.append(pltpu.make_async_copy(kr_hbm.at[layer, page], krbuf.at[slot, p], sem.at[slot, 1]))
        return out

    def fetch(chunk, slot):
        for c in copies(chunk, slot):
            c.start()

    def arrive(chunk, slot):
        for c in copies(chunk, slot):
            c.wait()

    @pl.when(s == 0)
    def _():
        fetch(0, 0)

    wkt = wkt_ref[...]
    q_lat = _dot(qn_ref[...], wkt)
    pad = jnp.zeros((MLA_DECODE_Q_ROWS - MLA_HEADS, MLA_KV_LORA), F32)
    lhs_ref[0:nope_w, :] = wkt
    lhs_ref[nope_w:, :] = jnp.concatenate([q_lat, pad], axis=0).astype(BF16)
    qr = qr_ref[...]

    def attend(slot, carry):
        m, l, acc = carry
        lhs = lhs_ref[...]
        cks, scs = [], []
        for p in range(0, pages, 2):
            ck = jnp.concatenate([ckbuf[slot, p], ckbuf[slot, p + 1]], axis=0).astype(BF16)
            kr = jnp.concatenate([krbuf[slot, p], krbuf[slot, p + 1]], axis=1)
            r = _dot_nt(lhs, ck)
            s_raw = r[nope_w:nope_w + MLA_HEADS] + _dot(qr, kr.astype(BF16))
            ss_rot = jnp.sum(kr * kr, axis=0, keepdims=True)
            ss = []
            for h in range(MLA_HEADS):
                x = r[h * MLA_NOPE:(h + 1) * MLA_NOPE]
                part = (x * x).reshape(MLA_NOPE // 8, 8, x.shape[1]).sum(axis=0)
                ss.append(jnp.sum(part, axis=0, keepdims=True) + ss_rot)
            ss = jnp.concatenate(ss, axis=0)
            scs.append(s_raw * lax.rsqrt(ss * (1.0 / MLA_QK) + EPS))
            cks.append(ck)
        sc = jnp.concatenate(scs, axis=1)
        m_new = jnp.maximum(m, jnp.max(sc, axis=-1, keepdims=True))
        alpha = jnp.exp(m - m_new)
        pr = jnp.exp(sc - m_new)
        l = l * alpha + jnp.sum(pr, axis=-1, keepdims=True)
        acc = acc * alpha + _dot(pr.astype(BF16), jnp.concatenate(cks, axis=0))
        return m_new, l, acc

    def chunk_pair(cc, carry):
        first = s * n_chunks + 2 * cc
        fetch(first + 1, 1)
        arrive(first, 0)
        carry = attend(0, carry)
        fetch(jnp.minimum(first + 2, last_chunk), 0)
        arrive(first + 1, 1)
        return attend(1, carry)

    start = (jnp.full((MLA_HEADS, 1), NEG_INF, F32), jnp.zeros((MLA_HEADS, 1), F32),
             jnp.zeros((MLA_HEADS, MLA_KV_LORA), F32))
    m, l, acc = lax.fori_loop(0, n_chunks // 2, chunk_pair, start)

    @pl.when(s == pl.num_programs(0) - 1)
    def _():
        arrive(last_chunk, 0)

    acc_ref[...] = acc
    m_ref[...] = jnp.broadcast_to(m, m_ref.shape)
    l_ref[...] = jnp.broadcast_to(l, l_ref.shape)


def mla_decode(qn, qr, w_uk_t, cache_ckv, krope_t, page_table, layer, *, pages=16):
    b, n_pages = page_table.shape
    assert n_pages % (2 * pages) == 0, "chunks are processed in pairs"
    nope_w = MLA_HEADS * MLA_NOPE
    per_seq = lambda w: pl.BlockSpec((None, MLA_HEADS, w), lambda s, pt: (s, 0, 0))
    grid_spec = pltpu.PrefetchScalarGridSpec(
        num_scalar_prefetch=1,
        grid=(b,),
        in_specs=[per_seq(nope_w), per_seq(MLA_ROPE),
                  pl.BlockSpec((nope_w, MLA_KV_LORA), lambda s, pt: (0, 0)),
                  pl.BlockSpec(memory_space=pl.ANY), pl.BlockSpec(memory_space=pl.ANY)],
        out_specs=[per_seq(MLA_KV_LORA), per_seq(LANES), per_seq(LANES)],
        scratch_shapes=[pltpu.VMEM((nope_w + MLA_DECODE_Q_ROWS, MLA_KV_LORA), BF16),
                        pltpu.VMEM((2, pages, PAGE_SIZE, MLA_KV_LORA), F32),
                        pltpu.VMEM((2, pages, MLA_ROPE, PAGE_SIZE), F32),
                        pltpu.SemaphoreType.DMA((2, 2))],
    )
    return pl.pallas_call(
        functools.partial(_mla_decode_kernel, layer=layer, n_pages=n_pages, pages=pages),
        grid_spec=grid_spec,
        out_shape=[jax.ShapeDtypeStruct((b, MLA_HEADS, MLA_KV_LORA), F32),
                   jax.ShapeDtypeStruct((b, MLA_HEADS, LANES), F32),
                   jax.ShapeDtypeStruct((b, MLA_HEADS, LANES), F32)],
        compiler_params=_params("arbitrary"),
        name="mla_decode",
    )(page_table.reshape(-1), qn, qr, w_uk_t, cache_ckv, krope_t)


def _mla_decode_finish_kernel(acc_ref, m_ref, l_ref, q_ref, k_ref, v_ref, wv_ref, o_ref):
    for h in range(MLA_HEADS):
        hl = slice(h * MLA_HEAD_PAD, (h + 1) * MLA_HEAD_PAD)
        vl = slice(h * MLA_VDIM, (h + 1) * MLA_VDIM)
        s_self = jnp.sum(q_ref[:, hl] * k_ref[:, hl], axis=-1, keepdims=True)
        m_past = m_ref[:, h * LANES:h * LANES + 1]
        l_past = l_ref[:, h * LANES:h * LANES + 1]
        m_new = jnp.maximum(m_past, s_self)
        a_past = jnp.exp(m_past - m_new)
        a_self = jnp.exp(s_self - m_new)
        lat = acc_ref[:, h * MLA_KV_LORA:(h + 1) * MLA_KV_LORA].astype(BF16)
        val = _dot(lat, wv_ref[:, vl])
        o_ref[:, vl] = ((val * a_past + v_ref[:, vl] * a_self) / (l_past * a_past + a_self)).astype(o_ref.dtype)


def mla_decode_finish(acc, m, l, q, k_self, v_self, w_uv):
    b = q.shape[0]
    return pl.pallas_call(
        _mla_decode_finish_kernel,
        out_shape=jax.ShapeDtypeStruct((b, MLA_WIDTH), BF16),
        name="mla_decode_finish",
    )(acc.reshape(b, -1), m.reshape(b, -1), l.reshape(b, -1), q, k_self, v_self, w_uv)


def _merge_kernel(x_ref, gl_ref, oa_ref, ob_ref, oc_ref, wa_ref, wb_ref, wc_ref, wo_ref, y_ref):
    merged = None
    for n, (o_ref, w_ref) in enumerate(((oa_ref, wa_ref), (ob_ref, wb_ref), (oc_ref, wc_ref))):
        gate = jax.nn.sigmoid(gl_ref[:, n * D_MODEL:(n + 1) * D_MODEL])
        term = gate * _dot(o_ref[...], w_ref[...])
        merged = term if merged is None else merged + term
    y_ref[...] = x_ref[...] + _dot(merged.astype(BF16), wo_ref[...])


def merge(x, z, o_a, o_b, o_c, w_a, w_b, w_c, w_out, *, tm):
    m = x.shape[0]
    row = lambda w: pl.BlockSpec((tm, w), lambda i: (i, 0))
    wspec = lambda r: pl.BlockSpec((r, D_MODEL), lambda i: (0, 0))
    return pl.pallas_call(
        _merge_kernel,
        grid=(m // tm,),
        in_specs=[row(D_MODEL), pl.BlockSpec((tm, N_BRANCH * D_MODEL), lambda i: (i, 1)),
                  row(SB_WIDTH), row(GM_WIDTH), row(MLA_WIDTH),
                  wspec(SB_WIDTH), wspec(GM_WIDTH), wspec(MLA_WIDTH), wspec(D_MODEL)],
        out_specs=row(D_MODEL),
        out_shape=jax.ShapeDtypeStruct((m, D_MODEL), F32),
        compiler_params=_params("parallel"),
        name="merge",
    )(x, z, o_a, o_b, o_c, w_a, w_b, w_c, w_out)


def _swiglu_tile(h, wg, wu, wd):
    a = _dot(h, wg)
    return _dot((a * jax.nn.sigmoid(a) * _dot(h, wu)).astype(BF16), wd)


def _ffn_kernel(x_ref, g_ref, wg_ref, wu_ref, wd_ref, y_ref, h_ref, acc_ref):
    f = pl.program_id(1)

    @pl.when(f == 0)
    def _():
        x = x_ref[...]
        ms = jnp.mean(x * x, axis=-1, keepdims=True)
        h_ref[...] = (x * lax.rsqrt(ms + EPS) * g_ref[...]).astype(BF16)
        acc_ref[...] = x

    acc_ref[...] += _swiglu_tile(h_ref[...], wg_ref[...], wu_ref[...], wd_ref[...])

    @pl.when(f == pl.num_programs(1) - 1)
    def _():
        y_ref[...] = acc_ref[...]


def ffn(x, g, w_gate, w_up, w_down, *, tm, tf):
    m = x.shape[0]
    d_ff = w_gate.shape[1]
    return pl.pallas_call(
        _ffn_kernel,
        grid=(m // tm, d_ff // tf),
        in_specs=[pl.BlockSpec((tm, D_MODEL), lambda i, f: (i, 0)),
                  pl.BlockSpec((1, D_MODEL), lambda i, f: (0, 0)),
                  pl.BlockSpec((D_MODEL, tf), lambda i, f: (0, f)),
                  pl.BlockSpec((D_MODEL, tf), lambda i, f: (0, f)),
                  pl.BlockSpec((tf, D_MODEL), lambda i, f: (f, 0))],
        out_specs=pl.BlockSpec((tm, D_MODEL), lambda i, f: (i, 0)),
        out_shape=jax.ShapeDtypeStruct((m, D_MODEL), F32),
        scratch_shapes=[pltpu.VMEM((tm, D_MODEL), BF16), pltpu.VMEM((tm, D_MODEL), F32)],
        compiler_params=_params("parallel", "arbitrary"),
        name="ffn",
    )(x, g.reshape(1, -1), w_gate, w_up, w_down)


def _router_kernel(x_ref, g_ref, wr_ref, h_ref, idx_ref, wt_ref):
    x = x_ref[...]
    h = x * lax.rsqrt(jnp.mean(x * x, axis=-1, keepdims=True) + EPS) * g_ref[...]
    h_ref[...] = h
    h1 = h.astype(BF16)
    r = h - h1.astype(F32)
    h2 = r.astype(BF16)
    h3 = (r - h2.astype(F32)).astype(BF16)
    w1, w2, w3 = wr_ref[0], wr_ref[1], wr_ref[2]
    logits = (_dot(h1, w1) + (_dot(h1, w2) + _dot(h2, w1)) + (_dot(h1, w3) + _dot(h2, w2) + _dot(h3, w1)))
    lane = lax.broadcasted_iota(jnp.int32, logits.shape, 1)
    logits = jnp.where(lane < N_EXPERTS, logits, -jnp.inf)
    m1 = jnp.max(logits, axis=-1, keepdims=True)
    i1 = jnp.min(jnp.where(logits == m1, lane, LANES), axis=-1, keepdims=True)
    rest = jnp.where(lane == i1, -jnp.inf, logits)
    m2 = jnp.max(rest, axis=-1, keepdims=True)
    i2 = jnp.min(jnp.where(rest == m2, lane, LANES), axis=-1, keepdims=True)
    e = jnp.exp(m2 - m1)
    p1 = 1.0 / (1.0 + e)
    p2 = e / (1.0 + e)
    idx_ref[...] = jnp.where(lane == 0, i1, jnp.where(lane == 1, i2, 0))
    wt_ref[...] = jnp.where(lane == 0, p1, jnp.where(lane == 1, p2, 0.0))


def router(x, g, w_router, *, tm):
    m = x.shape[0]
    w = jnp.pad(w_router, ((0, 0), (0, LANES - N_EXPERTS)))
    w1 = w.astype(BF16)
    r = w - w1.astype(F32)
    w2 = r.astype(BF16)
    w3 = (r - w2.astype(F32)).astype(BF16)
    row = lambda wd: pl.BlockSpec((tm, wd), lambda i: (i, 0))
    return pl.pallas_call(
        _router_kernel,
        grid=(m // tm,),
        in_specs=[row(D_MODEL), pl.BlockSpec((1, D_MODEL), lambda i: (0, 0)),
                  pl.BlockSpec((3, D_MODEL, LANES), lambda i: (0, 0, 0))],
        out_specs=[row(D_MODEL), row(LANES), row(LANES)],
        out_shape=[jax.ShapeDtypeStruct((m, D_MODEL), F32), jax.ShapeDtypeStruct((m, LANES), jnp.int32),
                   jax.ShapeDtypeStruct((m, LANES), F32)],
        compiler_params=_params("parallel"),
        name="router",
    )(x, g.reshape(1, -1), jnp.stack([w1, w2, w3]))


ROW_DMA_UNROLL = 8


def _experts_kernel(te_ref, nv_ref, src_ref, nxt_ref, h_hbm, wg_ref, wu_ref, wd_ref, y_ref,
                    hbuf, hb16, acc_ref, sem, *, tg):
    i, f = pl.program_id(0), pl.program_id(1)
    n_valid = nv_ref[0]
    valid = i < n_valid
    slot = i % 2
    share = tg // pl.num_programs(1)

    def row_copy(idx_ref, slot, r):
        return pltpu.make_async_copy(h_hbm.at[pl.ds(idx_ref[0, 0, r], 1), :],
                                     hbuf.at[slot, pl.ds(r, 1), :], sem.at[slot])

    def all_rows(idx_ref, slot, wait):
        def rows(g, c):
            for u in range(ROW_DMA_UNROLL):
                copy = row_copy(idx_ref, slot, g * ROW_DMA_UNROLL + u)
                if wait:
                    copy.wait()
                else:
                    copy.start()
            return c

        lax.fori_loop(0, tg // ROW_DMA_UNROLL, rows, 0)

    @pl.when(jnp.logical_and(i == 0, f == 0))
    def _():
        all_rows(src_ref, 0, wait=False)

    @pl.when(jnp.logical_and(i <= n_valid, f == 0))
    def _():
        all_rows(src_ref, slot, wait=True)
        hb16[...] = hbuf[slot].astype(BF16)
        acc_ref[...] = jnp.zeros_like(acc_ref)

    @pl.when(valid)
    def _():
        for u in range(share):
            row_copy(nxt_ref, 1 - slot, f * share + u).start()
        acc_ref[...] += _swiglu_tile(hb16[...], wg_ref[...], wu_ref[...], wd_ref[...])

    @pl.when(f == pl.num_programs(1) - 1)
    def _():
        y_ref[...] = jnp.where(valid, acc_ref[...], 0.0)


def experts(h, src, tile_expert, n_valid, w_gate, w_up, w_down, *, tg, tf):
    n_tiles = src.shape[0]
    d_ff = w_gate.shape[2]
    nf = d_ff // tf

    def w_index(i, f, te, nv):
        ok = i < nv[0]
        return te[i], jnp.where(ok, f, nf - 1)

    grid_spec = pltpu.PrefetchScalarGridSpec(
        num_scalar_prefetch=2,
        grid=(n_tiles, nf),
        in_specs=[pl.BlockSpec((1, 1, tg), lambda i, f, te, nv: (i, 0, 0), memory_space=pltpu.SMEM),
                  pl.BlockSpec((1, 1, tg), lambda i, f, te, nv: (jnp.minimum(i + 1, n_tiles - 1), 0, 0),
                               memory_space=pltpu.SMEM),
                  pl.BlockSpec(memory_space=pl.ANY),
                  pl.BlockSpec((None, D_MODEL, tf), lambda i, f, te, nv: (w_index(i, f, te, nv)[0], 0, w_index(i, f, te, nv)[1])),
                  pl.BlockSpec((None, D_MODEL, tf), lambda i, f, te, nv: (w_index(i, f, te, nv)[0], 0, w_index(i, f, te, nv)[1])),
                  pl.BlockSpec((None, tf, D_MODEL), lambda i, f, te, nv: (w_index(i, f, te, nv)[0], w_index(i, f, te, nv)[1], 0))],
        out_specs=pl.BlockSpec((tg, D_MODEL), lambda i, f, te, nv: (i, 0)),
        scratch_shapes=[pltpu.VMEM((2, tg, D_MODEL), F32), pltpu.VMEM((tg, D_MODEL), BF16),
                        pltpu.VMEM((tg, D_MODEL), F32), pltpu.SemaphoreType.DMA((2,))],
    )
    return pl.pallas_call(
        functools.partial(_experts_kernel, tg=tg),
        grid_spec=grid_spec,
        out_shape=jax.ShapeDtypeStruct((n_tiles * tg, D_MODEL), F32),
        compiler_params=_params("arbitrary", "arbitrary"),
        name="experts",
    )(tile_expert, n_valid, src, src, h, w_gate, w_up, w_down)


def _combine_kernel(pos_ref, x_ref, wt_ref, y_hbm, o_ref, ybuf, sem, *, tc):
    def gather(wait):
        def rows(g, c):
            for u in range(ROW_DMA_UNROLL):
                r = g * ROW_DMA_UNROLL + u
                for k in range(TOP_K):
                    copy = pltpu.make_async_copy(y_hbm.at[pl.ds(pos_ref[0, k, r], 1), :],
                                                 ybuf.at[k, pl.ds(r, 1), :], sem)
                    if wait:
                        copy.wait()
                    else:
                        copy.start()
            return c

        lax.fori_loop(0, tc // ROW_DMA_UNROLL, rows, 0)

    gather(wait=False)
    gather(wait=True)
    wt = wt_ref[...]
    o_ref[...] = x_ref[...] + (wt[:, 0:1] * ybuf[0] + wt[:, 1:2] * ybuf[1])


def combine(x, wt, pos, y, *, tc):
    m = x.shape[0]
    return pl.pallas_call(
        functools.partial(_combine_kernel, tc=tc),
        grid=(m // tc,),
        in_specs=[pl.BlockSpec((1, TOP_K, tc), lambda i: (i, 0, 0), memory_space=pltpu.SMEM),
                  pl.BlockSpec((tc, D_MODEL), lambda i: (i, 0)),
                  pl.BlockSpec((tc, LANES), lambda i: (i, 0)),
                  pl.BlockSpec(memory_space=pl.ANY)],
        out_specs=pl.BlockSpec((tc, D_MODEL), lambda i: (i, 0)),
        out_shape=jax.ShapeDtypeStruct((m, D_MODEL), F32),
        scratch_shapes=[pltpu.VMEM((TOP_K, tc, D_MODEL), F32), pltpu.SemaphoreType.DMA(())],
        compiler_params=_params("arbitrary"),
        name="combine",
    )(pos, x, wt, y)


def _route_plan(idx, *, tg):
    m = idx.shape[0]
    e_flat = idx.reshape(-1)
    onehot = (e_flat[:, None] == jnp.arange(N_EXPERTS)[None, :]).astype(jnp.int32)
    ranks = jnp.cumsum(onehot, axis=0) - onehot
    rank = jnp.sum(ranks * onehot, axis=1)
    counts = jnp.sum(onehot, axis=0)
    tiles = (counts + tg - 1) // tg
    tile_end = jnp.cumsum(tiles)
    start = (tile_end - tiles) * tg
    pos = start[e_flat] + rank
    n_tiles = (TOP_K * m) // tg + N_EXPERTS + 1
    src = jnp.zeros((n_tiles * tg,), jnp.int32).at[pos].set(jnp.arange(TOP_K * m, dtype=jnp.int32) // TOP_K)
    tile_ids = jnp.arange(n_tiles)
    tile_expert = jnp.minimum(jnp.sum(tile_ids[:, None] >= tile_end[None, :], axis=1), N_EXPERTS - 1)
    n_valid = tile_end[-1:]
    last = tile_expert[jnp.maximum(n_valid[0] - 1, 0)]
    tile_expert = jnp.where(tile_ids < n_valid[0], tile_expert, last).astype(jnp.int32)
    return src.reshape(n_tiles, 1, tg), tile_expert, n_valid.astype(jnp.int32), pos.reshape(m, TOP_K)


def moe(x, g, w_router, w_gate, w_up, w_down, *, tm, tg, tf):
    m = x.shape[0]
    h, idx, wt = router(x, g, w_router, tm=tm)
    src, tile_expert, n_valid, pos = _route_plan(idx[:, :TOP_K], tg=tg)
    y = experts(h, src, tile_expert, n_valid, w_gate, w_up, w_down, tg=tg, tf=tf)
    pos_t = pos.reshape(m // tm, tm, TOP_K).transpose(0, 2, 1)
    return combine(x, wt, pos_t, y, tc=tm)


def _rope_tables(pos):
    t = pos.shape[0]
    inv_freq = ROPE_THETA ** (-jnp.arange(0, MLA_ROPE, 2, dtype=F32) / MLA_ROPE)
    ang = pos[:, None] * inv_freq[None, :]
    c, s = jnp.cos(ang), jnp.sin(ang)
    half = MLA_ROPE // 2
    zeros = lambda w: jnp.zeros((t, w), F32)
    tail = LANES - MLA_QK
    return (jnp.concatenate([jnp.ones((t, MLA_NOPE), F32), c, c, zeros(tail)], axis=1),
            jnp.concatenate([zeros(MLA_NOPE), -s, zeros(half), zeros(tail)], axis=1),
            jnp.concatenate([zeros(MLA_NOPE), zeros(half), s, zeros(tail)], axis=1))


def _pad_heads(w, used):
    rows, heads, _ = w.shape
    return jnp.pad(w, ((0, 0), (0, 0), (0, MLA_HEAD_PAD - used))).reshape(rows, heads * MLA_HEAD_PAD)


def _layer_weights(l, w):
    d = {}
    cols = jnp.split(w['w_in'][l], np.cumsum([SB_WIDTH, SB_KV_WIDTH, SB_KV_WIDTH, GM_WIDTH, GM_WIDTH, MLA_Q_LORA,
                                              MLA_KV_LORA, MLA_ROPE]).tolist(), axis=1)
    zc = lambda n: jnp.zeros((D_MODEL, n), F32)
    krope = jnp.concatenate([zc(MLA_NOPE), cols[7], zc(LANES - MLA_QK)], axis=1)
    used = Z_KROPE + LANES
    d['w_in'] = jnp.concatenate(cols[:7] + [krope, zc(Z_MIX - used), cols[8]], axis=1).astype(BF16)
    d['g_mix'] = w['g_mix'][l]
    d['g_sb_q'] = (jnp.tile(w['g_sb_q'][l], SB_HEADS) * HEAD_DIM ** -0.5).reshape(1, -1)
    d['g_sb_k'] = jnp.tile(w['g_sb_k'][l], SB_KV_HEADS).reshape(1, -1)
    d['g_gm_v'] = w['g_gm_v'][l].reshape(1, -1)
    d['g_mla_cq'] = w['g_mla_cq'][l].reshape(1, -1)
    d['g_mla_ckv'] = w['g_mla_ckv'][l].reshape(1, -1)
    d['w_uq'] = _pad_heads(w['w_mla_uq'][l], MLA_QK).astype(BF16)
    d['w_uk_pad'] = _pad_heads(w['w_mla_uk'][l], MLA_NOPE).astype(BF16)
    d['w_uk_t'] = w['w_mla_uk'][l].reshape(MLA_KV_LORA, -1).T.astype(BF16)
    d['w_uv'] = w['w_mla_uv'][l].reshape(MLA_KV_LORA, -1).astype(BF16)
    rot_gain = jnp.concatenate([jnp.ones((MLA_ROPE,), F32), jnp.zeros((LANES - MLA_QK,), F32)])
    d['gain_q'] = (jnp.tile(jnp.concatenate([w['g_mla_q'][l], rot_gain]), MLA_HEADS) * MLA_QK ** -0.5).reshape(1, -1)
    d['gain_k'] = jnp.tile(jnp.concatenate([w['g_mla_k'][l], rot_gain]), MLA_HEADS).reshape(1, -1)
    d['g_mla_k'] = w['g_mla_k'][l]
    d['w_gm_s'] = w['w_gm_s'][l]
    d['b_gm_s'] = w['b_gm_s'][l]
    d['b_gm_full'] = jnp.repeat(w['b_gm_s'][l].T, GM_GROUP_DIM, axis=1)
    for name in ('w_br_a', 'w_br_b', 'w_br_c', 'w_out'):
        d[name] = w[name][l].astype(BF16)
    d['g_ffn'] = w['g_ffn'][l]
    i = l // 2
    if l % 2 == 0:
        for name in ('w_ffn_gate', 'w_ffn_up', 'w_ffn_down'):
            d[name] = w[name][i].astype(BF16)
    else:
        d['w_router'] = w['w_router'][i]
        for name in ('w_exp_gate', 'w_exp_up', 'w_exp_down'):
            d[name] = w[name][i].astype(BF16)
    return d


def _mixer_inputs(x, d, tabs, *, tm, qk_dtype):
    z = norm_matmul(x, d['g_mix'], d['w_in'], tm=min(x.shape[0], 1024), tn=Z_COLS // 4)
    q_a, k_a, v_a, u_b, v_b, cqn, ckv, krot = prep(z, d['g_sb_q'], d['g_sb_k'], d['g_gm_v'], d['g_mla_cq'],
                                                   d['g_mla_ckv'], tabs, tm=tm)
    q_c = mla_q(cqn, d['w_uq'], d['gain_q'], tabs, tm=tm, out_dtype=qk_dtype)
    k_c, v_c = mla_kv(ckv, krot, d['w_uk_pad'], d['w_uv'], d['gain_k'], tm=tm, out_dtype=qk_dtype)
    return z, q_a, k_a, v_a, u_b, v_b, q_c, k_c, v_c, ckv, krot


def _channel_mixer(x, l, d, *, tm, tg):
    if l % 2 == 0:
        return ffn(x, d['g_ffn'], d['w_ffn_gate'], d['w_ffn_up'], d['w_ffn_down'], tm=tm, tf=d['w_ffn_gate'].shape[1] // 2)
    return moe(x, d['g_ffn'], d['w_router'], d['w_exp_gate'], d['w_exp_up'], d['w_exp_down'], tm=tm, tg=tg,
               tf=d['w_exp_gate'].shape[2] // 4)


def _prompt_layer(x, l, d, tabs, batch):
    tm = 512
    z, q_a, k_a, v_a, u_b, v_b, q_c, k_c, v_c, ckv, krot = _mixer_inputs(x, d, tabs, tm=tm, qk_dtype=BF16)
    o_a = sb_prompt(q_a, k_a, v_a, batch=batch)
    o_b = gmlp(u_b, v_b, d['w_gm_s'], d['b_gm_full'])
    o_c = mla_prompt(q_c, k_c, v_c, batch=batch)
    x = merge(x, z, o_a, o_b, o_c, d['w_br_a'], d['w_br_b'], d['w_br_c'], d['w_out'], tm=tm)
    x = _channel_mixer(x, l, d, tm=tm, tg=512)
    return x, (k_a, v_a, ckv, krot[:, MLA_NOPE:MLA_QK])


def _sample_layer(x, l, d, tabs, caches, page_table):
    tm = x.shape[0]
    cache_sb_k, cache_sb_v, cache_ckv, cache_krope = caches
    z, q_a, k_a, v_a, u_b, v_b, q_c, k_c, v_c, ckv, krot = _mixer_inputs(x, d, tabs, tm=tm, qk_dtype=F32)
    kv_of_head = np.arange(SB_HEADS) // SB_GROUP
    lane_kv = np.arange(SB_KV_WIDTH) // HEAD_DIM
    head_mask = jnp.asarray(kv_of_head[:, None] == lane_kv[None, :], q_a.dtype)
    q_bd = jnp.tile(q_a.reshape(tm, SB_HEADS, 1, HEAD_DIM), (1, 1, SB_KV_HEADS, 1)).reshape(tm, SB_HEADS, SB_KV_WIDTH)
    o_full = sb_decode(q_bd * head_mask, cache_sb_k, cache_sb_v, page_table, l)
    o_full = o_full.reshape(tm, SB_KV_HEADS, SB_GROUP, SB_KV_HEADS, HEAD_DIM)
    o_a = jnp.stack([o_full[:, h, :, h, :] for h in range(SB_KV_HEADS)], axis=1).reshape(tm, SB_WIDTH).astype(BF16)
    o_b = gmlp_first(u_b, v_b, d['w_gm_s'], d['b_gm_s'])
    q_heads = q_c.reshape(tm, MLA_HEADS, MLA_HEAD_PAD)
    eye = jnp.eye(MLA_HEADS, dtype=F32)
    qn = (q_heads[:, :, :MLA_NOPE] * d['g_mla_k'])[:, :, None, :] * eye[None, :, :, None]
    qn = qn.reshape(tm, MLA_HEADS, MLA_HEADS * MLA_NOPE).astype(BF16)
    qr = q_heads[:, :, MLA_NOPE:MLA_QK].astype(BF16)
    acc, m_run, l_run = mla_decode(qn, qr, d['w_uk_t'], cache_ckv, cache_krope, page_table, l)
    o_c = mla_decode_finish(acc, m_run, l_run, q_c, k_c, v_c, d['w_uv'])
    x = merge(x, z, o_a, o_b, o_c, d['w_br_a'], d['w_br_b'], d['w_br_c'], d['w_out'], tm=tm)
    x = _channel_mixer(x, l, d, tm=tm, tg=128)
    return x, (k_a, v_a, ckv, krot[:, MLA_NOPE:MLA_QK], v_b)


def kernel(x_prompt, x_sample, cache_sb_k, cache_sb_v, cache_mla_ckv, cache_mla_krope, page_table, g_mix, w_in, g_sb_q, g_sb_k, g_gm_v, w_gm_s, b_gm_s, g_mla_cq, g_mla_ckv, w_mla_uq, w_mla_uk, w_mla_uv, g_mla_q, g_mla_k, w_br_a, w_br_b, w_br_c, w_out, g_ffn, w_ffn_gate, w_ffn_up, w_ffn_down, w_router, w_exp_gate, w_exp_up, w_exp_down):
    weights = dict(g_mix=g_mix, w_in=w_in, g_sb_q=g_sb_q, g_sb_k=g_sb_k, g_gm_v=g_gm_v, w_gm_s=w_gm_s, b_gm_s=b_gm_s,
                   g_mla_cq=g_mla_cq, g_mla_ckv=g_mla_ckv, w_mla_uq=w_mla_uq, w_mla_uk=w_mla_uk, w_mla_uv=w_mla_uv,
                   g_mla_q=g_mla_q, g_mla_k=g_mla_k, w_br_a=w_br_a, w_br_b=w_br_b, w_br_c=w_br_c, w_out=w_out,
                   g_ffn=g_ffn, w_ffn_gate=w_ffn_gate, w_ffn_up=w_ffn_up, w_ffn_down=w_ffn_down, w_router=w_router,
                   w_exp_gate=w_exp_gate, w_exp_up=w_exp_up, w_exp_down=w_exp_down)
    batch, seq, _ = x_prompt.shape
    dec_batch, dec_seq, _ = x_sample.shape
    depth = g_mix.shape[0]
    past_len = page_table.shape[1] * PAGE_SIZE
    assert dec_seq == 1, "the sample path handles one new token per sequence"
    tabs_p = _rope_tables(jnp.arange(seq, dtype=F32))
    tabs_s = tuple(jnp.broadcast_to(t, (dec_batch, LANES))
                   for t in _rope_tables(jnp.arange(dec_seq, dtype=F32) + past_len))
    caches = (sb_feature_major(cache_sb_k), sb_feature_major(cache_sb_v), cache_mla_ckv,
              jnp.transpose(cache_mla_krope, (0, 1, 3, 2)))
    h_p = x_prompt.reshape(batch * seq, D_MODEL)
    h_s = x_sample.reshape(dec_batch * dec_seq, D_MODEL)
    outs_p, outs_s = [], []
    for l in range(depth):
        d = _layer_weights(l, weights)
        h_p, new_p = _prompt_layer(h_p, l, d, tabs_p, batch)
        h_s, new_s = _sample_layer(h_s, l, d, tabs_s, caches, page_table)
        outs_p.append(new_p)
        outs_s.append(new_s)

    def stack(outs, n, shape):
        return jnp.stack([o[n].reshape(shape) for o in outs])

    kv_p = (batch, seq, SB_KV_HEADS, HEAD_DIM)
    kv_s = (dec_batch, dec_seq, SB_KV_HEADS, HEAD_DIM)
    return (h_p.reshape(batch, seq, D_MODEL), h_s.reshape(dec_batch, dec_seq, D_MODEL),
            stack(outs_p, 0, kv_p), stack(outs_p, 1, kv_p),
            stack(outs_p, 2, (batch, seq, MLA_KV_LORA)), stack(outs_p, 3, (batch, seq, MLA_ROPE)),
            stack(outs_s, 0, kv_s), stack(outs_s, 1, kv_s),
            stack(outs_s, 2, (dec_batch, dec_seq, MLA_KV_LORA)), stack(outs_s, 3, (dec_batch, dec_seq, MLA_ROPE)),
            stack(outs_s, 4, (dec_batch, dec_seq, GM_WIDTH)))
```

```python
import functools

import numpy as np
import jax
import jax.numpy as jnp
from jax import lax
from jax.experimental import pallas as pl
from jax.experimental.pallas import tpu as pltpu

F32 = jnp.float32
BF16 = jnp.bfloat16
EPS = 1e-6
NEG_INF = -1e30
LANES = 128

D_MODEL = 1024
HEAD_DIM = 64
SB_HEADS = 8
SB_KV_HEADS = 2
SB_GROUP = SB_HEADS // SB_KV_HEADS
SB_WIDTH = SB_HEADS * HEAD_DIM
SB_KV_WIDTH = SB_KV_HEADS * HEAD_DIM
GM_GROUPS = 8
GM_GROUP_DIM = 64
GM_CHUNK = 128
GM_WIDTH = GM_GROUPS * GM_GROUP_DIM
MLA_HEADS = 8
MLA_NOPE = 64
MLA_ROPE = 32
MLA_QK = MLA_NOPE + MLA_ROPE
MLA_VDIM = 64
MLA_Q_LORA = 768
MLA_KV_LORA = 256
MLA_WIDTH = MLA_HEADS * MLA_VDIM
MLA_HEAD_PAD = LANES
ROPE_THETA = 10000.0
N_BRANCH = 3
N_EXPERTS = 8
TOP_K = 2
PAGE_SIZE = 128

Z_SB_Q = 0
Z_SB_K = Z_SB_Q + SB_WIDTH
Z_SB_V = Z_SB_K + SB_KV_WIDTH
Z_GM_U = Z_SB_V + SB_KV_WIDTH
Z_GM_V = Z_GM_U + GM_WIDTH
Z_CQ = Z_GM_V + GM_WIDTH
Z_CKV = Z_CQ + MLA_Q_LORA
Z_KROPE = Z_CKV + MLA_KV_LORA
Z_MIX = 3 * D_MODEL
Z_COLS = Z_MIX + N_BRANCH * D_MODEL

VMEM_LIMIT = 56 * 1024 * 1024


def _params(*sem, vmem=VMEM_LIMIT):
    return pltpu.CompilerParams(dimension_semantics=sem, vmem_limit_bytes=vmem)


def _dot(a, b):
    return jnp.dot(a, b, preferred_element_type=F32)


def _dot_nt(a, b):
    return lax.dot_general(a, b, (((1,), (1,)), ((), ())), preferred_element_type=F32)


def _split_bf16(x):
    hi = x.astype(BF16)
    lo = (x - hi.astype(F32)).astype(BF16)
    return hi, lo


def _norm_matmul_kernel(x_ref, g_ref, w_ref, o_ref, h_ref):
    @pl.when(pl.program_id(1) == 0)
    def _():
        x = x_ref[...]
        ms = jnp.mean(x * x, axis=-1, keepdims=True)
        h_ref[...] = (x * lax.rsqrt(ms + EPS) * g_ref[...]).astype(BF16)

    o_ref[...] = _dot(h_ref[...], w_ref[...])


def norm_matmul(x, g, w, *, tm, tn):
    m, k = x.shape
    n = w.shape[1]
    return pl.pallas_call(
        _norm_matmul_kernel,
        grid=(m // tm, n // tn),
        in_specs=[pl.BlockSpec((tm, k), lambda i, j: (i, 0)),
                  pl.BlockSpec((1, k), lambda i, j: (0, 0)),
                  pl.BlockSpec((k, tn), lambda i, j: (0, j))],
        out_specs=pl.BlockSpec((tm, tn), lambda i, j: (i, j)),
        out_shape=jax.ShapeDtypeStruct((m, n), F32),
        scratch_shapes=[pltpu.VMEM((tm, k), BF16)],
        compiler_params=_params("parallel", "arbitrary"),
        name="norm_matmul",
    )(x, g.reshape(1, k), w)


def _gelu(x):
    c = np.float32(np.sqrt(2.0 / np.pi))
    return x * (0.5 * (1.0 + jnp.tanh(c * (x + 0.044715 * (x * x * x)))))


def _rope(x, c, s1, s2, width):
    return x * c + pltpu.roll(x, width - MLA_ROPE // 2, 1) * s1 + pltpu.roll(x, MLA_ROPE // 2, 1) * s2


def _group_sumsq(x, bd):
    hi, lo = _split_bf16(x * x)
    return _dot(hi, bd) + _dot(lo, bd)


def _prep_kernel(z_ref, gq_ref, gk_ref, gv_ref, gcq_ref, gckv_ref, c_ref, s1_ref, s2_ref, bdq_ref, bdk_ref,
                 qa_ref, ka_ref, va_ref, ub_ref, vb_ref, cqn_ref, ckv_ref, krot_ref):
    inv_d = 1.0 / HEAD_DIM
    q = z_ref[:, Z_SB_Q:Z_SB_Q + SB_WIDTH]
    qa_ref[...] = (q * lax.rsqrt(_group_sumsq(q, bdq_ref[...]) * inv_d + EPS) * gq_ref[...]).astype(qa_ref.dtype)
    k = z_ref[:, Z_SB_K:Z_SB_K + SB_KV_WIDTH]
    ka_ref[...] = k * lax.rsqrt(_group_sumsq(k, bdk_ref[...]) * inv_d + EPS) * gk_ref[...]
    va_ref[...] = z_ref[:, Z_SB_V:Z_SB_V + SB_KV_WIDTH]
    ub_ref[...] = _gelu(z_ref[:, Z_GM_U:Z_GM_U + GM_WIDTH]).astype(ub_ref.dtype)
    gv = _gelu(z_ref[:, Z_GM_V:Z_GM_V + GM_WIDTH])
    vb_ref[...] = gv * lax.rsqrt(jnp.mean(gv * gv, axis=-1, keepdims=True) + EPS) * gv_ref[...]
    cq = z_ref[:, Z_CQ:Z_CQ + MLA_Q_LORA]
    cqn_ref[...] = (cq * lax.rsqrt(jnp.mean(cq * cq, axis=-1, keepdims=True) + EPS) * gcq_ref[...]).astype(cqn_ref.dtype)
    ckv = z_ref[:, Z_CKV:Z_CKV + MLA_KV_LORA]
    ckv_ref[...] = ckv * lax.rsqrt(jnp.mean(ckv * ckv, axis=-1, keepdims=True) + EPS) * gckv_ref[...]
    kr = z_ref[:, Z_KROPE:Z_KROPE + LANES]
    krot_ref[...] = _rope(kr, c_ref[...], s1_ref[...], s2_ref[...], LANES)


def _block_diag_ones(width, group):
    idx = np.arange(width) // group
    return jnp.asarray(idx[:, None] == idx[None, :], BF16)


def prep(z, gq, gk, gv, gcq, gckv, tabs, *, tm):
    m = z.shape[0]
    nt = tabs[0].shape[0] // tm
    row = lambda w: pl.BlockSpec((tm, w), lambda i: (i, 0))
    vec = lambda w: pl.BlockSpec((1, w), lambda i: (0, 0))
    tab = pl.BlockSpec((tm, LANES), lambda i: (i % nt, 0))
    full = lambda w: pl.BlockSpec((w, w), lambda i: (0, 0))
    outs = [(SB_WIDTH, BF16), (SB_KV_WIDTH, F32), (SB_KV_WIDTH, F32), (GM_WIDTH, BF16), (GM_WIDTH, F32),
            (MLA_Q_LORA, BF16), (MLA_KV_LORA, F32), (LANES, F32)]
    return pl.pallas_call(
        _prep_kernel,
        grid=(m // tm,),
        in_specs=[pl.BlockSpec((tm, Z_MIX), lambda i: (i, 0)),
                  vec(SB_WIDTH), vec(SB_KV_WIDTH), vec(GM_WIDTH), vec(MLA_Q_LORA), vec(MLA_KV_LORA),
                  tab, tab, tab, full(SB_WIDTH), full(SB_KV_WIDTH)],
        out_specs=[row(w) for w, _ in outs],
        out_shape=[jax.ShapeDtypeStruct((m, w), dt) for w, dt in outs],
        compiler_params=_params("parallel"),
        name="prep",
    )(z, gq, gk, gv, gcq, gckv, *tabs, _block_diag_ones(SB_WIDTH, HEAD_DIM), _block_diag_ones(SB_KV_WIDTH, HEAD_DIM))


def _head_norm(xh, gain):
    ss = jnp.sum(xh * xh, axis=-1, keepdims=True) * (1.0 / MLA_QK)
    return xh * lax.rsqrt(ss + EPS) * gain


def _mla_q_kernel(cq_ref, w_ref, gain_ref, c_ref, s1_ref, s2_ref, o_ref):
    q = _dot(cq_ref[...], w_ref[...])
    c, s1, s2 = c_ref[...], s1_ref[...], s2_ref[...]
    for h in range(MLA_HEADS):
        sl = slice(h * MLA_HEAD_PAD, (h + 1) * MLA_HEAD_PAD)
        xh = _rope(q[:, sl], c, s1, s2, LANES)
        o_ref[:, sl] = _head_norm(xh, gain_ref[:, sl]).astype(o_ref.dtype)


def mla_q(cqn, w_uq_pad, gain, tabs, *, tm, out_dtype):
    m = cqn.shape[0]
    nt = tabs[0].shape[0] // tm
    width = MLA_HEADS * MLA_HEAD_PAD
    tab = pl.BlockSpec((tm, LANES), lambda i: (i % nt, 0))
    return pl.pallas_call(
        _mla_q_kernel,
        grid=(m // tm,),
        in_specs=[pl.BlockSpec((tm, MLA_Q_LORA), lambda i: (i, 0)),
                  pl.BlockSpec((MLA_Q_LORA, width), lambda i: (0, 0)),
                  pl.BlockSpec((1, width), lambda i: (0, 0)), tab, tab, tab],
        out_specs=pl.BlockSpec((tm, width), lambda i: (i, 0)),
        out_shape=jax.ShapeDtypeStruct((m, width), out_dtype),
        compiler_params=_params("parallel"),
        name="mla_q",
    )(cqn, w_uq_pad, gain, *tabs)


def _mla_kv_kernel(ckv_ref, krot_ref, wk_ref, wv_ref, gain_ref, k_ref, v_ref):
    ckv = ckv_ref[...].astype(BF16)
    kn = _dot(ckv, wk_ref[...])
    krot = krot_ref[...]
    for h in range(MLA_HEADS):
        sl = slice(h * MLA_HEAD_PAD, (h + 1) * MLA_HEAD_PAD)
        k_ref[:, sl] = _head_norm(kn[:, sl] + krot, gain_ref[:, sl]).astype(k_ref.dtype)
    v_ref[...] = _dot(ckv, wv_ref[...]).astype(v_ref.dtype)


def mla_kv(ckv, krot, w_uk_pad, w_uv, gain, *, tm, out_dtype):
    m = ckv.shape[0]
    width = MLA_HEADS * MLA_HEAD_PAD
    return pl.pallas_call(
        _mla_kv_kernel,
        grid=(m // tm,),
        in_specs=[pl.BlockSpec((tm, MLA_KV_LORA), lambda i: (i, 0)),
                  pl.BlockSpec((tm, LANES), lambda i: (i, 0)),
                  pl.BlockSpec((MLA_KV_LORA, width), lambda i: (0, 0)),
                  pl.BlockSpec((MLA_KV_LORA, MLA_WIDTH), lambda i: (0, 0)),
                  pl.BlockSpec((1, width), lambda i: (0, 0))],
        out_specs=[pl.BlockSpec((tm, width), lambda i: (i, 0)), pl.BlockSpec((tm, MLA_WIDTH), lambda i: (i, 0))],
        out_shape=[jax.ShapeDtypeStruct((m, width), out_dtype), jax.ShapeDtypeStruct((m, MLA_WIDTH), out_dtype)],
        compiler_params=_params("parallel"),
        name="mla_kv",
    )(ckv, krot, w_uk_pad, w_uv, gain)


KEEP_FLOOR = -104.0


def _sb_block(z, mask, keep, u):
    l1p = jnp.log1p(jnp.exp(-jnp.abs(z)))
    log_beta = jnp.minimum(z, 0.0) - l1p
    log_keep = jnp.minimum(-z, 0.0) - l1p
    if mask is not None:
        log_keep = jnp.where(mask, log_keep, 0.0)
    hi, lo = _split_bf16(log_keep)
    after = _dot(hi, u) + _dot(lo, u)
    w = jnp.exp(log_beta + after + keep)
    if mask is not None:
        w = jnp.where(mask, w, 0.0)
    return w, keep + (after[:, 0:1] + log_keep[:, 0:1])


def _later_keys_ones(n):
    return jnp.asarray(np.arange(n)[:, None] > np.arange(n)[None, :], BF16)


def _sb_prompt_kernel(q_ref, k_ref, v_ref, u_ref, o_ref, *, bq):
    i = pl.program_id(1)
    u = u_ref[...]
    rows = SB_GROUP * bq
    r = lax.broadcasted_iota(jnp.int32, (rows, bq), 0) & (bq - 1)
    s = lax.broadcasted_iota(jnp.int32, (rows, bq), 1)
    diag_mask = s < r
    kv_heads = range(SB_KV_HEADS)
    lanes = [slice(h * HEAD_DIM, (h + 1) * HEAD_DIM) for h in kv_heads]
    qs = [jnp.concatenate([q_ref[:, (h * SB_GROUP + g) * HEAD_DIM:(h * SB_GROUP + g + 1) * HEAD_DIM]
                           for g in range(SB_GROUP)], axis=0) for h in kv_heads]

    def block(j, mask, accs, keeps):
        new_accs, new_keeps = [], []
        for h in kv_heads:
            kb = k_ref[pl.ds(j * bq, bq), lanes[h]].astype(BF16)
            vb = v_ref[pl.ds(j * bq, bq), lanes[h]].astype(BF16)
            w, keep = _sb_block(_dot_nt(qs[h], kb), mask, keeps[h], u)
            pv = _dot(w.astype(BF16), vb)
            new_accs.append(pv if accs is None else accs[h] + pv)
            new_keeps.append(keep)
        return tuple(new_accs), tuple(new_keeps)

    accs, keeps = block(i, diag_mask, None, (jnp.zeros((rows, 1), F32),) * SB_KV_HEADS)

    def more(carry):
        jj, _, keeps = carry
        return jnp.logical_and(jj <= i, jnp.max(functools.reduce(jnp.maximum, keeps)) > KEEP_FLOOR)

    def body(carry):
        jj, accs, keeps = carry
        accs, keeps = block(i - jj, None, accs, keeps)
        return jj + 1, accs, keeps

    _, accs, _ = lax.while_loop(more, body, (jnp.int32(1), accs, keeps))
    for h in kv_heads:
        for g in range(SB_GROUP):
            col = (h * SB_GROUP + g) * HEAD_DIM
            o_ref[:, col:col + HEAD_DIM] = accs[h][g * bq:(g + 1) * bq].astype(o_ref.dtype)


def sb_prompt(q, k, v, *, batch, bq=128):
    m = q.shape[0]
    t = m // batch
    nq = t // bq
    return pl.pallas_call(
        functools.partial(_sb_prompt_kernel, bq=bq),
        grid=(batch, nq),
        in_specs=[pl.BlockSpec((bq, SB_WIDTH), lambda b, i: (b * nq + i, 0)),
                  pl.BlockSpec((t, SB_KV_WIDTH), lambda b, i: (b, 0)),
                  pl.BlockSpec((t, SB_KV_WIDTH), lambda b, i: (b, 0)),
                  pl.BlockSpec((bq, bq), lambda b, i: (0, 0))],
        out_specs=pl.BlockSpec((bq, SB_WIDTH), lambda b, i: (b * nq + i, 0)),
        out_shape=jax.ShapeDtypeStruct((m, SB_WIDTH), BF16),
        compiler_params=_params("parallel", "arbitrary"),
        name="sb_prompt",
    )(q, k, v, _later_keys_ones(bq))


def _sb_decode_kernel(pt_ref, q_ref, u_ref, k_hbm, v_hbm, o_ref, kbuf, vbuf, sem, *, layer, n_pages, pages):
    s = pl.program_id(0)
    slot = s % 2

    def copies(seq, chunk, slot):
        out = []
        for p in range(pages):
            page = pt_ref[seq * n_pages + n_pages - 1 - (chunk * pages + p)]
            out.append(pltpu.make_async_copy(k_hbm.at[layer, page], kbuf.at[slot, p], sem.at[slot, 0]))
            out.append(pltpu.make_async_copy(v_hbm.at[layer, page], vbuf.at[slot, p], sem.at[slot, 1]))
        return out

    def fetch(seq, chunk, slot):
        for c in copies(seq, chunk, slot):
            c.start()

    def arrive(seq, chunk, slot):
        for c in copies(seq, chunk, slot):
            c.wait()

    @pl.when(s == 0)
    def _():
        fetch(s, 0, slot)

    @pl.when(s + 1 < pl.num_programs(0))
    def _():
        fetch(s + 1, 0, 1 - slot)

    q = q_ref[...]
    u = u_ref[...]

    def attend(acc, keep):
        for p in range(pages):
            w, keep = _sb_block(_dot(q, kbuf[slot, p].astype(BF16)), None, keep, u)
            acc = acc + _dot_nt(w.astype(BF16), vbuf[slot, p].astype(BF16))
        return acc, keep

    arrive(s, 0, slot)
    acc, keep = attend(jnp.zeros(o_ref.shape, F32), jnp.zeros((SB_HEADS, 1), F32))

    def more(carry):
        chunk, _, keep = carry
        return jnp.logical_and(chunk < n_pages // pages, jnp.max(keep) > KEEP_FLOOR)

    def body(carry):
        chunk, acc, keep = carry
        fetch(s, chunk, slot)
        arrive(s, chunk, slot)
        acc, keep = attend(acc, keep)
        return chunk + 1, acc, keep

    _, acc, _ = lax.while_loop(more, body, (jnp.int32(1), acc, keep))
    o_ref[...] = acc


def sb_feature_major(cache):
    return jnp.transpose(cache, (0, 1, 3, 4, 2)).reshape(cache.shape[0], cache.shape[1], SB_KV_WIDTH, PAGE_SIZE)


def sb_decode(q_bd, cache_k_t, cache_v_t, page_table, layer, *, pages=2):
    b, n_pages = page_table.shape
    grid_spec = pltpu.PrefetchScalarGridSpec(
        num_scalar_prefetch=1,
        grid=(b,),
        in_specs=[pl.BlockSpec((None, SB_HEADS, SB_KV_WIDTH), lambda s, pt: (s, 0, 0)),
                  pl.BlockSpec((PAGE_SIZE, PAGE_SIZE), lambda s, pt: (0, 0)),
                  pl.BlockSpec(memory_space=pl.ANY), pl.BlockSpec(memory_space=pl.ANY)],
        out_specs=pl.BlockSpec((None, SB_HEADS, SB_KV_WIDTH), lambda s, pt: (s, 0, 0)),
        scratch_shapes=[pltpu.VMEM((2, pages, SB_KV_WIDTH, PAGE_SIZE), F32),
                        pltpu.VMEM((2, pages, SB_KV_WIDTH, PAGE_SIZE), F32),
                        pltpu.SemaphoreType.DMA((2, 2))],
    )
    return pl.pallas_call(
        functools.partial(_sb_decode_kernel, layer=layer, n_pages=n_pages, pages=pages),
        grid_spec=grid_spec,
        out_shape=jax.ShapeDtypeStruct((b, SB_HEADS, SB_KV_WIDTH), F32),
        compiler_params=_params("arbitrary"),
        name="sb_decode",
    )(page_table.reshape(-1), q_bd, _later_keys_ones(PAGE_SIZE), cache_k_t, cache_v_t)


def _gmlp_kernel(u_ref, v_ref, w_ref, b_ref, o_ref, *, chunks):
    t = lax.broadcasted_iota(jnp.int32, (GM_CHUNK, GM_CHUNK), 0)
    s = lax.broadcasted_iota(jnp.int32, (GM_CHUNK, GM_CHUNK), 1)
    causal = s <= t
    group = lax.broadcasted_iota(jnp.int32, (GM_CHUNK, GM_WIDTH), 1) // GM_GROUP_DIM
    ws = [jnp.where(causal, w_ref[g], 0.0).astype(BF16) for g in range(GM_GROUPS)]
    for c in range(chunks):
        rows = slice(c * GM_CHUNK, (c + 1) * GM_CHUNK)
        vc = v_ref[rows, :].astype(BF16)
        mixed = b_ref[...]
        for g in range(GM_GROUPS):
            mixed = mixed + jnp.where(group == g, _dot(ws[g], vc), 0.0)
        o_ref[rows, :] = (u_ref[rows, :].astype(F32) * mixed).astype(o_ref.dtype)


def gmlp(u, v, w_s, b_full, *, chunks=4):
    m = u.shape[0]
    tm = chunks * GM_CHUNK
    return pl.pallas_call(
        functools.partial(_gmlp_kernel, chunks=chunks),
        grid=(m // tm,),
        in_specs=[pl.BlockSpec((tm, GM_WIDTH), lambda i: (i, 0)),
                  pl.BlockSpec((tm, GM_WIDTH), lambda i: (i, 0)),
                  pl.BlockSpec((GM_GROUPS, GM_CHUNK, GM_CHUNK), lambda i: (0, 0, 0)),
                  pl.BlockSpec((GM_CHUNK, GM_WIDTH), lambda i: (0, 0))],
        out_specs=pl.BlockSpec((tm, GM_WIDTH), lambda i: (i, 0)),
        out_shape=jax.ShapeDtypeStruct((m, GM_WIDTH), BF16),
        compiler_params=_params("parallel"),
        name="gmlp",
    )(u, v, w_s, b_full)


def _gmlp_first_kernel(u_ref, v_ref, w_ref, b_ref, o_ref):
    o_ref[...] = (u_ref[...].astype(F32) * (w_ref[...] * v_ref[...] + b_ref[...])).astype(o_ref.dtype)


def gmlp_first(u, v, w_s, b_s):
    m = u.shape[0]
    w0 = jnp.repeat(w_s[:, 0, 0], GM_GROUP_DIM).reshape(1, GM_WIDTH)
    b0 = jnp.repeat(b_s[:, 0], GM_GROUP_DIM).reshape(1, GM_WIDTH)
    return pl.pallas_call(
        _gmlp_first_kernel,
        out_shape=jax.ShapeDtypeStruct((m, GM_WIDTH), BF16),
        name="gmlp_first",
    )(u, v, w0, b0)


def _mla_prompt_kernel(q_ref, k_ref, v_ref, o_ref, *, bq):
    i = pl.program_id(2)
    t = lax.broadcasted_iota(jnp.int32, (bq, bq), 0)
    s = lax.broadcasted_iota(jnp.int32, (bq, bq), 1)
    causal = s <= t
    heads = range(2)
    qk_lanes = [slice(hh * MLA_HEAD_PAD, (hh + 1) * MLA_HEAD_PAD) for hh in heads]
    v_lanes = [slice(hh * MLA_VDIM, (hh + 1) * MLA_VDIM) for hh in heads]

    def key_block(j, carry, mask):
        out = []
        for hh in heads:
            m, l, acc = carry[hh]
            sc = _dot_nt(q_ref[:, qk_lanes[hh]], k_ref[pl.ds(j * bq, bq), qk_lanes[hh]])
            if mask is not None:
                sc = jnp.where(mask, sc, NEG_INF)
            m_new = jnp.maximum(m, jnp.max(sc, axis=-1, keepdims=True))
            alpha = jnp.exp(m - m_new)
            p = jnp.exp(sc - m_new)
            pv = _dot(p.astype(BF16), v_ref[pl.ds(j * bq, bq), v_lanes[hh]])
            out.append((m_new, l * alpha + jnp.sum(p, axis=-1, keepdims=True), acc * alpha + pv))
        return tuple(out)

    start = (jnp.full((bq, 1), NEG_INF, F32), jnp.zeros((bq, 1), F32), jnp.zeros((bq, MLA_VDIM), F32))
    carry = lax.fori_loop(0, i, lambda j, c: key_block(j, c, None), (start, start))
    carry = key_block(i, carry, causal)
    for hh in heads:
        _, l, acc = carry[hh]
        o_ref[:, v_lanes[hh]] = (acc / l).astype(o_ref.dtype)


def mla_prompt(q, k, v, *, batch, bq=512):
    m = q.shape[0]
    t = m // batch
    nq = t // bq
    return pl.pallas_call(
        functools.partial(_mla_prompt_kernel, bq=bq),
        grid=(batch, MLA_HEADS // 2, nq),
        in_specs=[pl.BlockSpec((bq, 2 * MLA_HEAD_PAD), lambda b, h, i: (b * nq + i, h)),
                  pl.BlockSpec((t, 2 * MLA_HEAD_PAD), lambda b, h, i: (b, h)),
                  pl.BlockSpec((t, 2 * MLA_VDIM), lambda b, h, i: (b, h))],
        out_specs=pl.BlockSpec((bq, 2 * MLA_VDIM), lambda b, h, i: (b * nq + i, h)),
        out_shape=jax.ShapeDtypeStruct((m, MLA_WIDTH), BF16),
        compiler_params=_params("parallel", "parallel", "arbitrary"),
        name="mla_prompt",
    )(q, k, v)


MLA_DECODE_Q_ROWS = 16


def _mla_decode_kernel(pt_ref, qn_ref, qr_ref, wkt_ref, ckv_hbm, kr_hbm, acc_ref, m_ref, l_ref,
                       lhs_ref, ckbuf, krbuf, sem, *, layer, n_pages, pages):
    s = pl.program_id(0)
    n_chunks = n_pages // pages
    last_chunk = pl.num_programs(0) * n_chunks - 1
    nope_w = MLA_HEADS * MLA_NOPE

    def copies(chunk, slot):
        out = []
        for p in range(pages):
            page = pt_ref[chunk * pages + p]
            out.append(pltpu.make_async_copy(ckv_hbm.at[layer, page], ckbuf.at[slot, p], sem.at[slot, 0]))
            out.append(pltpu.make_async_copy(kr_hbm.at[layer, page], krbuf.at[slot, p], sem.at[slot, 1]))
        return out

    def fetch(chunk, slot):
        for c in copies(chunk, slot):
            c.start()

    def arrive(chunk, slot):
        for c in copies(chunk, slot):
            c.wait()

    @pl.when(s == 0)
    def _():
        fetch(0, 0)

    wkt = wkt_ref[...]
    q_lat = _dot(qn_ref[...], wkt)
    pad = jnp.zeros((MLA_DECODE_Q_ROWS - MLA_HEADS, MLA_KV_LORA), F32)
    lhs_ref[0:nope_w, :] = wkt
    lhs_ref[nope_w:, :] = jnp.concatenate([q_lat, pad], axis=0).astype(BF16)
    qr = qr_ref[...]

    def attend(slot, carry):
        m, l, acc = carry
        lhs = lhs_ref[...]
        cks, scs = [], []
        for p in range(0, pages, 2):
            ck = jnp.concatenate([ckbuf[slot, p], ckbuf[slot, p + 1]], axis=0).astype(BF16)
            kr = jnp.concatenate([krbuf[slot, p], krbuf[slot, p + 1]], axis=1)
            r = _dot_nt(lhs, ck)
            s_raw = r[nope_w:nope_w + MLA_HEADS] + _dot(qr, kr.astype(BF16))
            ss_rot = jnp.sum(kr * kr, axis=0, keepdims=True)
            ss = []
            for h in range(MLA_HEADS):
                x = r[h * MLA_NOPE:(h + 1) * MLA_NOPE]
                part = (x * x).reshape(MLA_NOPE // 8, 8, x.shape[1]).sum(axis=0)
                ss.append(jnp.sum(part, axis=0, keepdims=True) + ss_rot)
            ss = jnp.concatenate(ss, axis=0)
            scs.append(s_raw * lax.rsqrt(ss * (1.0 / MLA_QK) + EPS))
            cks.append(ck)
        sc = jnp.concatenate(scs, axis=1)
        m_new = jnp.maximum(m, jnp.max(sc, axis=-1, keepdims=True))
        alpha = jnp.exp(m - m_new)
        pr = jnp.exp(sc - m_new)
        l = l * alpha + jnp.sum(pr, axis=-1, keepdims=True)
        acc = acc * alpha + _dot(pr.astype(BF16), jnp.concatenate(cks, axis=0))
        return m_new, l, acc

    def chunk_pair(cc, carry):
        first = s * n_chunks + 2 * cc
        fetch(first + 1, 1)
        arrive(first, 0)
        carry = attend(0, carry)
        fetch(jnp.minimum(first + 2, last_chunk), 0)
        arrive(first + 1, 1)
        return attend(1, carry)

    start = (jnp.full((MLA_HEADS, 1), NEG_INF, F32), jnp.zeros((MLA_HEADS, 1), F32),
             jnp.zeros((MLA_HEADS, MLA_KV_LORA), F32))
    m, l, acc = lax.fori_loop(0, n_chunks // 2, chunk_pair, start)

    @pl.when(s == pl.num_programs(0) - 1)
    def _():
        arrive(last_chunk, 0)

    acc_ref[...] = acc
    m_ref[...] = jnp.broadcast_to(m, m_ref.shape)
    l_ref[...] = jnp.broadcast_to(l, l_ref.shape)


def mla_decode(qn, qr, w_uk_t, cache_ckv, krope_t, page_table, layer, *, pages=32):
    b, n_pages = page_table.shape
    assert n_pages % (2 * pages) == 0, "chunks are processed in pairs"
    nope_w = MLA_HEADS * MLA_NOPE
    per_seq = lambda w: pl.BlockSpec((None, MLA_HEADS, w), lambda s, pt: (s, 0, 0))
    grid_spec = pltpu.PrefetchScalarGridSpec(
        num_scalar_prefetch=1,
        grid=(b,),
        in_specs=[per_seq(nope_w), per_seq(MLA_ROPE),
                  pl.BlockSpec((nope_w, MLA_KV_LORA), lambda s, pt: (0, 0)),
                  pl.BlockSpec(memory_space=pl.ANY), pl.BlockSpec(memory_space=pl.ANY)],
        out_specs=[per_seq(MLA_KV_LORA), per_seq(LANES), per_seq(LANES)],
        scratch_shapes=[pltpu.VMEM((nope_w + MLA_DECODE_Q_ROWS, MLA_KV_LORA), BF16),
                        pltpu.VMEM((2, pages, PAGE_SIZE, MLA_KV_LORA), F32),
                        pltpu.VMEM((2, pages, MLA_ROPE, PAGE_SIZE), F32),
                        pltpu.SemaphoreType.DMA((2, 2))],
    )
    return pl.pallas_call(
        functools.partial(_mla_decode_kernel, layer=layer, n_pages=n_pages, pages=pages),
        grid_spec=grid_spec,
        out_shape=[jax.ShapeDtypeStruct((b, MLA_HEADS, MLA_KV_LORA), F32),
                   jax.ShapeDtypeStruct((b, MLA_HEADS, LANES), F32),
                   jax.ShapeDtypeStruct((b, MLA_HEADS, LANES), F32)],
        compiler_params=_params("arbitrary"),
        name="mla_decode",
    )(page_table.reshape(-1), qn, qr, w_uk_t, cache_ckv, krope_t)


def _mla_decode_finish_kernel(acc_ref, m_ref, l_ref, q_ref, k_ref, v_ref, wv_ref, o_ref):
    for h in range(MLA_HEADS):
        hl = slice(h * MLA_HEAD_PAD, (h + 1) * MLA_HEAD_PAD)
        vl = slice(h * MLA_VDIM, (h + 1) * MLA_VDIM)
        s_self = jnp.sum(q_ref[:, hl] * k_ref[:, hl], axis=-1, keepdims=True)
        m_past = m_ref[:, h * LANES:h * LANES + 1]
        l_past = l_ref[:, h * LANES:h * LANES + 1]
        m_new = jnp.maximum(m_past, s_self)
        a_past = jnp.exp(m_past - m_new)
        a_self = jnp.exp(s_self - m_new)
        lat = acc_ref[:, h * MLA_KV_LORA:(h + 1) * MLA_KV_LORA].astype(BF16)
        val = _dot(lat, wv_ref[:, vl])
        o_ref[:, vl] = ((val * a_past + v_ref[:, vl] * a_self) / (l_past * a_past + a_self)).astype(o_ref.dtype)


def mla_decode_finish(acc, m, l, q, k_self, v_self, w_uv):
    b = q.shape[0]
    return pl.pallas_call(
        _mla_decode_finish_kernel,
        out_shape=jax.ShapeDtypeStruct((b, MLA_WIDTH), BF16),
        name="mla_decode_finish",
    )(acc.reshape(b, -1), m.reshape(b, -1), l.reshape(b, -1), q, k_self, v_self, w_uv)


def _merge_kernel(x_ref, gl_ref, oa_ref, ob_ref, oc_ref, wa_ref, wb_ref, wc_ref, wo_ref, y_ref):
    merged = None
    for n, (o_ref, w_ref) in enumerate(((oa_ref, wa_ref), (ob_ref, wb_ref), (oc_ref, wc_ref))):
        gate = jax.nn.sigmoid(gl_ref[:, n * D_MODEL:(n + 1) * D_MODEL])
        term = gate * _dot(o_ref[...], w_ref[...])
        merged = term if merged is None else merged + term
    y_ref[...] = x_ref[...] + _dot(merged.astype(BF16), wo_ref[...])


def merge(x, z, o_a, o_b, o_c, w_a, w_b, w_c, w_out, *, tm):
    m = x.shape[0]
    row = lambda w: pl.BlockSpec((tm, w), lambda i: (i, 0))
    wspec = lambda r: pl.BlockSpec((r, D_MODEL), lambda i: (0, 0))
    return pl.pallas_call(
        _merge_kernel,
        grid=(m // tm,),
        in_specs=[row(D_MODEL), pl.BlockSpec((tm, N_BRANCH * D_MODEL), lambda i: (i, 1)),
                  row(SB_WIDTH), row(GM_WIDTH), row(MLA_WIDTH),
                  wspec(SB_WIDTH), wspec(GM_WIDTH), wspec(MLA_WIDTH), wspec(D_MODEL)],
        out_specs=row(D_MODEL),
        out_shape=jax.ShapeDtypeStruct((m, D_MODEL), F32),
        compiler_params=_params("parallel"),
        name="merge",
    )(x, z, o_a, o_b, o_c, w_a, w_b, w_c, w_out)


def _swiglu_tile(h, wg, wu, wd):
    a = _dot(h, wg)
    return _dot((a * jax.nn.sigmoid(a) * _dot(h, wu)).astype(BF16), wd)


def _ffn_kernel(x_ref, g_ref, wg_ref, wu_ref, wd_ref, y_ref, h_ref, acc_ref):
    f = pl.program_id(1)

    @pl.when(f == 0)
    def _():
        x = x_ref[...]
        ms = jnp.mean(x * x, axis=-1, keepdims=True)
        h_ref[...] = (x * lax.rsqrt(ms + EPS) * g_ref[...]).astype(BF16)
        acc_ref[...] = x

    acc_ref[...] += _swiglu_tile(h_ref[...], wg_ref[...], wu_ref[...], wd_ref[...])

    @pl.when(f == pl.num_programs(1) - 1)
    def _():
        y_ref[...] = acc_ref[...]


def ffn(x, g, w_gate, w_up, w_down, *, tm, tf):
    m = x.shape[0]
    d_ff = w_gate.shape[1]
    return pl.pallas_call(
        _ffn_kernel,
        grid=(m // tm, d_ff // tf),
        in_specs=[pl.BlockSpec((tm, D_MODEL), lambda i, f: (i, 0)),
                  pl.BlockSpec((1, D_MODEL), lambda i, f: (0, 0)),
                  pl.BlockSpec((D_MODEL, tf), lambda i, f: (0, f)),
                  pl.BlockSpec((D_MODEL, tf), lambda i, f: (0, f)),
                  pl.BlockSpec((tf, D_MODEL), lambda i, f: (f, 0))],
        out_specs=pl.BlockSpec((tm, D_MODEL), lambda i, f: (i, 0)),
        out_shape=jax.ShapeDtypeStruct((m, D_MODEL), F32),
        scratch_shapes=[pltpu.VMEM((tm, D_MODEL), BF16), pltpu.VMEM((tm, D_MODEL), F32)],
        compiler_params=_params("parallel", "arbitrary"),
        name="ffn",
    )(x, g.reshape(1, -1), w_gate, w_up, w_down)


def _router_kernel(x_ref, g_ref, wr_ref, h_ref, idx_ref, wt_ref):
    x = x_ref[...]
    h = x * lax.rsqrt(jnp.mean(x * x, axis=-1, keepdims=True) + EPS) * g_ref[...]
    h_ref[...] = h
    h1 = h.astype(BF16)
    r = h - h1.astype(F32)
    h2 = r.astype(BF16)
    h3 = (r - h2.astype(F32)).astype(BF16)
    w1, w2, w3 = wr_ref[0], wr_ref[1], wr_ref[2]
    logits = (_dot(h1, w1) + (_dot(h1, w2) + _dot(h2, w1)) + (_dot(h1, w3) + _dot(h2, w2) + _dot(h3, w1)))
    lane = lax.broadcasted_iota(jnp.int32, logits.shape, 1)
    logits = jnp.where(lane < N_EXPERTS, logits, -jnp.inf)
    m1 = jnp.max(logits, axis=-1, keepdims=True)
    i1 = jnp.min(jnp.where(logits == m1, lane, LANES), axis=-1, keepdims=True)
    rest = jnp.where(lane == i1, -jnp.inf, logits)
    m2 = jnp.max(rest, axis=-1, keepdims=True)
    i2 = jnp.min(jnp.where(rest == m2, lane, LANES), axis=-1, keepdims=True)
    e = jnp.exp(m2 - m1)
    p1 = 1.0 / (1.0 + e)
    p2 = e / (1.0 + e)
    idx_ref[...] = jnp.where(lane == 0, i1, jnp.where(lane == 1, i2, 0))
    wt_ref[...] = jnp.where(lane == 0, p1, jnp.where(lane == 1, p2, 0.0))


def router(x, g, w_router, *, tm):
    m = x.shape[0]
    w = jnp.pad(w_router, ((0, 0), (0, LANES - N_EXPERTS)))
    w1 = w.astype(BF16)
    r = w - w1.astype(F32)
    w2 = r.astype(BF16)
    w3 = (r - w2.astype(F32)).astype(BF16)
    row = lambda wd: pl.BlockSpec((tm, wd), lambda i: (i, 0))
    return pl.pallas_call(
        _router_kernel,
        grid=(m // tm,),
        in_specs=[row(D_MODEL), pl.BlockSpec((1, D_MODEL), lambda i: (0, 0)),
                  pl.BlockSpec((3, D_MODEL, LANES), lambda i: (0, 0, 0))],
        out_specs=[row(D_MODEL), row(LANES), row(LANES)],
        out_shape=[jax.ShapeDtypeStruct((m, D_MODEL), F32), jax.ShapeDtypeStruct((m, LANES), jnp.int32),
                   jax.ShapeDtypeStruct((m, LANES), F32)],
        compiler_params=_params("parallel"),
        name="router",
    )(x, g.reshape(1, -1), jnp.stack([w1, w2, w3]))


ROW_DMA_UNROLL = 8


def _experts_kernel(te_ref, nv_ref, src_ref, nxt_ref, h_hbm, wg_ref, wu_ref, wd_ref, y_ref,
                    hbuf, hb16, acc_ref, sem, *, tg):
    i, f = pl.program_id(0), pl.program_id(1)
    n_valid = nv_ref[0]
    valid = i < n_valid
    slot = i % 2
    share = tg // pl.num_programs(1)

    def row_copy(idx_ref, slot, r):
        return pltpu.make_async_copy(h_hbm.at[pl.ds(idx_ref[0, 0, r], 1), :],
                                     hbuf.at[slot, pl.ds(r, 1), :], sem.at[slot])

    def all_rows(idx_ref, slot, wait):
        def rows(g, c):
            for u in range(ROW_DMA_UNROLL):
                copy = row_copy(idx_ref, slot, g * ROW_DMA_UNROLL + u)
                if wait:
                    copy.wait()
                else:
                    copy.start()
            return c

        lax.fori_loop(0, tg // ROW_DMA_UNROLL, rows, 0)

    @pl.when(jnp.logical_and(i == 0, f == 0))
    def _():
        all_rows(src_ref, 0, wait=False)

    @pl.when(jnp.logical_and(i <= n_valid, f == 0))
    def _():
        all_rows(src_ref, slot, wait=True)
        hb16[...] = hbuf[slot].astype(BF16)
        acc_ref[...] = jnp.zeros_like(acc_ref)

    @pl.when(valid)
    def _():
        for u in range(share):
            row_copy(nxt_ref, 1 - slot, f * share + u).start()
        acc_ref[...] += _swiglu_tile(hb16[...], wg_ref[...], wu_ref[...], wd_ref[...])

    @pl.when(f == pl.num_programs(1) - 1)
    def _():
        y_ref[...] = jnp.where(valid, acc_ref[...], 0.0)


def experts(h, src, tile_expert, n_valid, w_gate, w_up, w_down, *, tg, tf):
    n_tiles = src.shape[0]
    d_ff = w_gate.shape[2]
    nf = d_ff // tf

    def w_index(i, f, te, nv):
        ok = i < nv[0]
        return te[i], jnp.where(ok, f, nf - 1)

    grid_spec = pltpu.PrefetchScalarGridSpec(
        num_scalar_prefetch=2,
        grid=(n_tiles, nf),
        in_specs=[pl.BlockSpec((1, 1, tg), lambda i, f, te, nv: (i, 0, 0), memory_space=pltpu.SMEM),
                  pl.BlockSpec((1, 1, tg), lambda i, f, te, nv: (jnp.minimum(i + 1, n_tiles - 1), 0, 0),
                               memory_space=pltpu.SMEM),
                  pl.BlockSpec(memory_space=pl.ANY),
                  pl.BlockSpec((None, D_MODEL, tf), lambda i, f, te, nv: (w_index(i, f, te, nv)[0], 0, w_index(i, f, te, nv)[1])),
                  pl.BlockSpec((None, D_MODEL, tf), lambda i, f, te, nv: (w_index(i, f, te, nv)[0], 0, w_index(i, f, te, nv)[1])),
                  pl.BlockSpec((None, tf, D_MODEL), lambda i, f, te, nv: (w_index(i, f, te, nv)[0], w_index(i, f, te, nv)[1], 0))],
        out_specs=pl.BlockSpec((tg, D_MODEL), lambda i, f, te, nv: (i, 0)),
        scratch_shapes=[pltpu.VMEM((2, tg, D_MODEL), F32), pltpu.VMEM((tg, D_MODEL), BF16),
                        pltpu.VMEM((tg, D_MODEL), F32), pltpu.SemaphoreType.DMA((2,))],
    )
    return pl.pallas_call(
        functools.partial(_experts_kernel, tg=tg),
        grid_spec=grid_spec,
        out_shape=jax.ShapeDtypeStruct((n_tiles * tg, D_MODEL), F32),
        compiler_params=_params("arbitrary", "arbitrary"),
        name="experts",
    )(tile_expert, n_valid, src, src, h, w_gate, w_up, w_down)


def _combine_kernel(pos_ref, x_ref, wt_ref, y_hbm, o_ref, ybuf, sem, *, tc):
    def gather(wait):
        def rows(g, c):
            for u in range(ROW_DMA_UNROLL):
                r = g * ROW_DMA_UNROLL + u
                for k in range(TOP_K):
                    copy = pltpu.make_async_copy(y_hbm.at[pl.ds(pos_ref[0, k, r], 1), :],
                                                 ybuf.at[k, pl.ds(r, 1), :], sem)
                    if wait:
                        copy.wait()
                    else:
                        copy.start()
            return c

        lax.fori_loop(0, tc // ROW_DMA_UNROLL, rows, 0)

    gather(wait=False)
    gather(wait=True)
    wt = wt_ref[...]
    o_ref[...] = x_ref[...] + (wt[:, 0:1] * ybuf[0] + wt[:, 1:2] * ybuf[1])


def combine(x, wt, pos, y, *, tc):
    m = x.shape[0]
    return pl.pallas_call(
        functools.partial(_combine_kernel, tc=tc),
        grid=(m // tc,),
        in_specs=[pl.BlockSpec((1, TOP_K, tc), lambda i: (i, 0, 0), memory_space=pltpu.SMEM),
                  pl.BlockSpec((tc, D_MODEL), lambda i: (i, 0)),
                  pl.BlockSpec((tc, LANES), lambda i: (i, 0)),
                  pl.BlockSpec(memory_space=pl.ANY)],
        out_specs=pl.BlockSpec((tc, D_MODEL), lambda i: (i, 0)),
        out_shape=jax.ShapeDtypeStruct((m, D_MODEL), F32),
        scratch_shapes=[pltpu.VMEM((TOP_K, tc, D_MODEL), F32), pltpu.SemaphoreType.DMA(())],
        compiler_params=_params("arbitrary"),
        name="combine",
    )(pos, x, wt, y)


def _route_plan(idx, *, tg):
    m = idx.shape[0]
    e_flat = idx.reshape(-1)
    onehot = (e_flat[:, None] == jnp.arange(N_EXPERTS)[None, :]).astype(jnp.int32)
    ranks = jnp.cumsum(onehot, axis=0) - onehot
    rank = jnp.sum(ranks * onehot, axis=1)
    counts = jnp.sum(onehot, axis=0)
    tiles = (counts + tg - 1) // tg
    tile_end = jnp.cumsum(tiles)
    start = (tile_end - tiles) * tg
    pos = start[e_flat] + rank
    n_tiles = (TOP_K * m) // tg + N_EXPERTS + 1
    src = jnp.zeros((n_tiles * tg,), jnp.int32).at[pos].set(jnp.arange(TOP_K * m, dtype=jnp.int32) // TOP_K)
    tile_ids = jnp.arange(n_tiles)
    tile_expert = jnp.minimum(jnp.sum(tile_ids[:, None] >= tile_end[None, :], axis=1), N_EXPERTS - 1)
    n_valid = tile_end[-1:]
    last = tile_expert[jnp.maximum(n_valid[0] - 1, 0)]
    tile_expert = jnp.where(tile_ids < n_valid[0], tile_expert, last).astype(jnp.int32)
    return src.reshape(n_tiles, 1, tg), tile_expert, n_valid.astype(jnp.int32), pos.reshape(m, TOP_K)


def moe(x, g, w_router, w_gate, w_up, w_down, *, tm, tg, tf):
    m = x.shape[0]
    h, idx, wt = router(x, g, w_router, tm=tm)
    src, tile_expert, n_valid, pos = _route_plan(idx[:, :TOP_K], tg=tg)
    y = experts(h, src, tile_expert, n_valid, w_gate, w_up, w_down, tg=tg, tf=tf)
    pos_t = pos.reshape(m // tm, tm, TOP_K).transpose(0, 2, 1)
    return combine(x, wt, pos_t, y, tc=tm)


def _rope_tables(pos):
    t = pos.shape[0]
    inv_freq = ROPE_THETA ** (-jnp.arange(0, MLA_ROPE, 2, dtype=F32) / MLA_ROPE)
    ang = pos[:, None] * inv_freq[None, :]
    c, s = jnp.cos(ang), jnp.sin(ang)
    half = MLA_ROPE // 2
    zeros = lambda w: jnp.zeros((t, w), F32)
    tail = LANES - MLA_QK
    return (jnp.concatenate([jnp.ones((t, MLA_NOPE), F32), c, c, zeros(tail)], axis=1),
            jnp.concatenate([zeros(MLA_NOPE), -s, zeros(half), zeros(tail)], axis=1),
            jnp.concatenate([zeros(MLA_NOPE), zeros(half), s, zeros(tail)], axis=1))


def _pad_heads(w, used):
    rows, heads, _ = w.shape
    return jnp.pad(w, ((0, 0), (0, 0), (0, MLA_HEAD_PAD - used))).reshape(rows, heads * MLA_HEAD_PAD)


def _layer_weights(l, w):
    d = {}
    cols = jnp.split(w['w_in'][l], np.cumsum([SB_WIDTH, SB_KV_WIDTH, SB_KV_WIDTH, GM_WIDTH, GM_WIDTH, MLA_Q_LORA,
                                              MLA_KV_LORA, MLA_ROPE]).tolist(), axis=1)
    zc = lambda n: jnp.zeros((D_MODEL, n), F32)
    krope = jnp.concatenate([zc(MLA_NOPE), cols[7], zc(LANES - MLA_QK)], axis=1)
    used = Z_KROPE + LANES
    d['w_in'] = jnp.concatenate(cols[:7] + [krope, zc(Z_MIX - used), cols[8]], axis=1).astype(BF16)
    d['g_mix'] = w['g_mix'][l]
    d['g_sb_q'] = (jnp.tile(w['g_sb_q'][l], SB_HEADS) * HEAD_DIM ** -0.5).reshape(1, -1)
    d['g_sb_k'] = jnp.tile(w['g_sb_k'][l], SB_KV_HEADS).reshape(1, -1)
    d['g_gm_v'] = w['g_gm_v'][l].reshape(1, -1)
    d['g_mla_cq'] = w['g_mla_cq'][l].reshape(1, -1)
    d['g_mla_ckv'] = w['g_mla_ckv'][l].reshape(1, -1)
    d['w_uq'] = _pad_heads(w['w_mla_uq'][l], MLA_QK).astype(BF16)
    d['w_uk_pad'] = _pad_heads(w['w_mla_uk'][l], MLA_NOPE).astype(BF16)
    d['w_uk_t'] = w['w_mla_uk'][l].reshape(MLA_KV_LORA, -1).T.astype(BF16)
    d['w_uv'] = w['w_mla_uv'][l].reshape(MLA_KV_LORA, -1).astype(BF16)
    rot_gain = jnp.concatenate([jnp.ones((MLA_ROPE,), F32), jnp.zeros((LANES - MLA_QK,), F32)])
    d['gain_q'] = (jnp.tile(jnp.concatenate([w['g_mla_q'][l], rot_gain]), MLA_HEADS) * MLA_QK ** -0.5).reshape(1, -1)
    d['gain_k'] = jnp.tile(jnp.concatenate([w['g_mla_k'][l], rot_gain]), MLA_HEADS).reshape(1, -1)
    d['g_mla_k'] = w['g_mla_k'][l]
    d['w_gm_s'] = w['w_gm_s'][l]
    d['b_gm_s'] = w['b_gm_s'][l]
    d['b_gm_full'] = jnp.repeat(w['b_gm_s'][l].T, GM_GROUP_DIM, axis=1)
    for name in ('w_br_a', 'w_br_b', 'w_br_c', 'w_out'):
        d[name] = w[name][l].astype(BF16)
    d['g_ffn'] = w['g_ffn'][l]
    i = l // 2
    if l % 2 == 0:
        for name in ('w_ffn_gate', 'w_ffn_up', 'w_ffn_down'):
            d[name] = w[name][i].astype(BF16)
    else:
        d['w_router'] = w['w_router'][i]
        for name in ('w_exp_gate', 'w_exp_up', 'w_exp_down'):
            d[name] = w[name][i].astype(BF16)
    return d


def _mixer_inputs(x, d, tabs, *, tm, qk_dtype):
    z = norm_matmul(x, d['g_mix'], d['w_in'], tm=min(x.shape[0], 1024), tn=Z_COLS // 4)
    q_a, k_a, v_a, u_b, v_b, cqn, ckv, krot = prep(z, d['g_sb_q'], d['g_sb_k'], d['g_gm_v'], d['g_mla_cq'],
                                                   d['g_mla_ckv'], tabs, tm=tm)
    q_c = mla_q(cqn, d['w_uq'], d['gain_q'], tabs, tm=tm, out_dtype=qk_dtype)
    k_c, v_c = mla_kv(ckv, krot, d['w_uk_pad'], d['w_uv'], d['gain_k'], tm=tm, out_dtype=qk_dtype)
    return z, q_a, k_a, v_a, u_b, v_b, q_c, k_c, v_c, ckv, krot


def _channel_mixer(x, l, d, *, tm, tg):
    if l % 2 == 0:
        return ffn(x, d['g_ffn'], d['w_ffn_gate'], d['w_ffn_up'], d['w_ffn_down'], tm=tm, tf=d['w_ffn_gate'].shape[1] // 2)
    return moe(x, d['g_ffn'], d['w_router'], d['w_exp_gate'], d['w_exp_up'], d['w_exp_down'], tm=tm, tg=tg,
               tf=d['w_exp_gate'].shape[2] // 4)


def _prompt_layer(x, l, d, tabs, batch):
    tm = 512
    z, q_a, k_a, v_a, u_b, v_b, q_c, k_c, v_c, ckv, krot = _mixer_inputs(x, d, tabs, tm=tm, qk_dtype=BF16)
    o_a = sb_prompt(q_a, k_a, v_a, batch=batch)
    o_b = gmlp(u_b, v_b, d['w_gm_s'], d['b_gm_full'])
    o_c = mla_prompt(q_c, k_c, v_c, batch=batch)
    x = merge(x, z, o_a, o_b, o_c, d['w_br_a'], d['w_br_b'], d['w_br_c'], d['w_out'], tm=tm)
    x = _channel_mixer(x, l, d, tm=tm, tg=512)
    return x, (k_a, v_a, ckv, krot[:, MLA_NOPE:MLA_QK])


def _sample_layer(x, l, d, tabs, caches, page_table):
    tm = x.shape[0]
    cache_sb_k, cache_sb_v, cache_ckv, cache_krope = caches
    z, q_a, k_a, v_a, u_b, v_b, q_c, k_c, v_c, ckv, krot = _mixer_inputs(x, d, tabs, tm=tm, qk_dtype=F32)
    kv_of_head = np.arange(SB_HEADS) // SB_GROUP
    lane_kv = np.arange(SB_KV_WIDTH) // HEAD_DIM
    head_mask = jnp.asarray(kv_of_head[:, None] == lane_kv[None, :], q_a.dtype)
    q_bd = jnp.tile(q_a.reshape(tm, SB_HEADS, 1, HEAD_DIM), (1, 1, SB_KV_HEADS, 1)).reshape(tm, SB_HEADS, SB_KV_WIDTH)
    o_full = sb_decode(q_bd * head_mask, cache_sb_k, cache_sb_v, page_table, l)
    o_full = o_full.reshape(tm, SB_KV_HEADS, SB_GROUP, SB_KV_HEADS, HEAD_DIM)
    o_a = jnp.stack([o_full[:, h, :, h, :] for h in range(SB_KV_HEADS)], axis=1).reshape(tm, SB_WIDTH).astype(BF16)
    o_b = gmlp_first(u_b, v_b, d['w_gm_s'], d['b_gm_s'])
    q_heads = q_c.reshape(tm, MLA_HEADS, MLA_HEAD_PAD)
    eye = jnp.eye(MLA_HEADS, dtype=F32)
    qn = (q_heads[:, :, :MLA_NOPE] * d['g_mla_k'])[:, :, None, :] * eye[None, :, :, None]
    qn = qn.reshape(tm, MLA_HEADS, MLA_HEADS * MLA_NOPE).astype(BF16)
    qr = q_heads[:, :, MLA_NOPE:MLA_QK].astype(BF16)
    acc, m_run, l_run = mla_decode(qn, qr, d['w_uk_t'], cache_ckv, cache_krope, page_table, l)
    o_c = mla_decode_finish(acc, m_run, l_run, q_c, k_c, v_c, d['w_uv'])
    x = merge(x, z, o_a, o_b, o_c, d['w_br_a'], d['w_br_b'], d['w_br_c'], d['w_out'], tm=tm)
    x = _channel_mixer(x, l, d, tm=tm, tg=128)
    return x, (k_a, v_a, ckv, krot[:, MLA_NOPE:MLA_QK], v_b)


def kernel(x_prompt, x_sample, cache_sb_k, cache_sb_v, cache_mla_ckv, cache_mla_krope, page_table, g_mix, w_in, g_sb_q, g_sb_k, g_gm_v, w_gm_s, b_gm_s, g_mla_cq, g_mla_ckv, w_mla_uq, w_mla_uk, w_mla_uv, g_mla_q, g_mla_k, w_br_a, w_br_b, w_br_c, w_out, g_ffn, w_ffn_gate, w_ffn_up, w_ffn_down, w_router, w_exp_gate, w_exp_up, w_exp_down):
    weights = dict(g_mix=g_mix, w_in=w_in, g_sb_q=g_sb_q, g_sb_k=g_sb_k, g_gm_v=g_gm_v, w_gm_s=w_gm_s, b_gm_s=b_gm_s,
                   g_mla_cq=g_mla_cq, g_mla_ckv=g_mla_ckv, w_mla_uq=w_mla_uq, w_mla_uk=w_mla_uk, w_mla_uv=w_mla_uv,
                   g_mla_q=g_mla_q, g_mla_k=g_mla_k, w_br_a=w_br_a, w_br_b=w_br_b, w_br_c=w_br_c, w_out=w_out,
                   g_ffn=g_ffn, w_ffn_gate=w_ffn_gate, w_ffn_up=w_ffn_up, w_ffn_down=w_ffn_down, w_router=w_router,
                   w_exp_gate=w_exp_gate, w_exp_up=w_exp_up, w_exp_down=w_exp_down)
    batch, seq, _ = x_prompt.shape
    dec_batch, dec_seq, _ = x_sample.shape
    depth = g_mix.shape[0]
    past_len = page_table.shape[1] * PAGE_SIZE
    assert dec_seq == 1, "the sample path handles one new token per sequence"
    tabs_p = _rope_tables(jnp.arange(seq, dtype=F32))
    tabs_s = tuple(jnp.broadcast_to(t, (dec_batch, LANES))
                   for t in _rope_tables(jnp.arange(dec_seq, dtype=F32) + past_len))
    caches = (sb_feature_major(cache_sb_k), sb_feature_major(cache_sb_v), cache_mla_ckv,
              jnp.transpose(cache_mla_krope, (0, 1, 3, 2)))
    h_p = x_prompt.reshape(batch * seq, D_MODEL)
    h_s = x_sample.reshape(dec_batch * dec_seq, D_MODEL)
    outs_p, outs_s = [], []
    for l in range(depth):
        d = _layer_weights(l, weights)
        h_p, new_p = _prompt_layer(h_p, l, d, tabs_p, batch)
        h_s, new_s = _sample_layer(h_s, l, d, tabs_s, caches, page_table)
        outs_p.append(new_p)
        outs_s.append(new_s)

    def stack(outs, n, shape):
        return jnp.stack([o[n].reshape(shape) for o in outs])

    kv_p = (batch, seq, SB_KV_HEADS, HEAD_DIM)
    kv_s = (dec_batch, dec_seq, SB_KV_HEADS, HEAD_DIM)
    return (h_p.reshape(batch, seq, D_MODEL), h_s.reshape(dec_batch, dec_seq, D_MODEL),
            stack(outs_p, 0, kv_p), stack(outs_p, 1, kv_p),
            stack(outs_p, 2, (batch, seq, MLA_KV_LORA)), stack(outs_p, 3, (batch, seq, MLA_ROPE)),
            stack(outs_s, 0, kv_s), stack(outs_s, 1, kv_s),
            stack(outs_s, 2, (dec_batch, dec_seq, MLA_KV_LORA)), stack(outs_s, 3, (dec_batch, dec_seq, MLA_ROPE)),
            stack(outs_s, 4, (dec_batch, dec_seq, GM_WIDTH)))
```
